```python
import numpy as np
import jax
import jax.numpy as jnp
from jax import lax

D_MODEL = 1024
BATCH = 8
SEQ = 4096
DEPTH = 2

D_MIX = D_MODEL
NSA_HEADS = 8
NSA_KV_GROUPS = 2
NSA_HEAD_DIM = 64
NSA_WIDTH = NSA_HEADS * NSA_HEAD_DIM
NSA_KV_WIDTH = NSA_KV_GROUPS * NSA_HEAD_DIM
CMP_BLOCK = 32
CMP_STRIDE = 16
CMP_HIDDEN = 2 * NSA_HEAD_DIM
SLC_BLOCK = 64
SLC_TOPN = 16
WINDOW = 512
Q_BLOCK = 64
FORCE_BONUS = 1e4
M_HEADS = 4
M_WIDTH = D_MIX - NSA_WIDTH
M_HEAD_DIM = M_WIDTH // M_HEADS
M_CHUNK = 64
CONV_WIDTH = 4
FFN_HIDDEN = ((8 * D_MODEL + 3 * 256 - 1) // (3 * 256)) * 256
D_IN_PROJ = NSA_WIDTH + 6 * NSA_KV_WIDTH + 3 * NSA_HEADS + 4 * M_WIDTH + 2 * M_HEADS
RMS_EPS = 1e-6
NEG_INF = -1e30

kernel_name = "hybrid_nsa_mlstm_block"


def rmsnorm(x, g):
    xf = x.astype(jnp.float32)
    y = xf * lax.rsqrt(jnp.mean(xf * xf, axis=-1, keepdims=True) + RMS_EPS)
    return (y * g.astype(jnp.float32)).astype(x.dtype)


def in_proj_sizes():
    return ([NSA_WIDTH] + [NSA_KV_WIDTH] * 6 + [3 * NSA_HEADS]
            + [M_WIDTH] * 3 + [M_HEADS, M_HEADS, M_WIDTH])


def compress_blocks(a, pe, w1, w2):
    B, G, T, dh = a.shape
    ac = a.reshape(B, G, T // CMP_STRIDE, CMP_STRIDE, dh)
    blocks = jnp.concatenate([ac[:, :, :-1], ac[:, :, 1:]], axis=3) + pe
    flat = blocks.reshape(B, G, T // CMP_STRIDE - 1, CMP_BLOCK * dh)
    return jax.nn.silu(flat @ w1) @ w2


def nsa_mixer(q, kvs, gates, q_g, k_g, pe, w1, w2, out_g):
    B, T, _ = q.shape
    H, G, dh = NSA_HEADS, NSA_KV_GROUPS, NSA_HEAD_DIM
    HG = H // G
    scale = dh ** -0.5
    q = rmsnorm(q.reshape(B, T, G, HG, dh), q_g).transpose(0, 2, 3, 1, 4)
    k_c, v_c, k_s, v_s, k_w, v_w = [a.reshape(B, T, G, dh).transpose(0, 2, 1, 3) for a in kvs]
    k_s = rmsnorm(k_s, k_g[1])
    k_w = rmsnorm(k_w, k_g[2])
    t_pos = jnp.arange(T)

    k_cmp = rmsnorm(compress_blocks(k_c, pe[0], w1[0], w2[0]), k_g[0])
    v_cmp = compress_blocks(v_c, pe[1], w1[1], w2[1])
    n_cmp = T // CMP_STRIDE - 1
    cmp_start = jnp.arange(n_cmp) * CMP_STRIDE
    cmp_end = cmp_start + CMP_BLOCK - 1
    cmp_valid = cmp_end[None, :] <= t_pos[:, None]
    s = jnp.einsum("bghtd,bgnd->bghtn", q, k_cmp).astype(jnp.float32) * scale
    p_cmp = jax.nn.softmax(jnp.where(cmp_valid, s, NEG_INF), axis=-1) * cmp_valid
    o_cmp = jnp.einsum("bghtn,bgnd->bghtd", p_cmp.astype(v_cmp.dtype), v_cmp)

    n_blk = T // SLC_BLOCK
    n_sel = min(SLC_TOPN, n_blk)
    blk = jnp.arange(n_blk)
    overlap = ((cmp_start[:, None] < (blk[None, :] + 1) * SLC_BLOCK)
               & (cmp_end[:, None] >= blk[None, :] * SLC_BLOCK)).astype(jnp.float32)
    imp = jnp.einsum("bghtn,nj->bgtj", p_cmp, overlap)
    cur = t_pos // SLC_BLOCK
    forced = (blk[None] == 0) | (blk[None] == cur[:, None]) | (blk[None] == cur[:, None] - 1)
    sel_valid = blk[None] * SLC_BLOCK <= t_pos[:, None]
    score = jnp.where(sel_valid, imp + FORCE_BONUS * forced.astype(jnp.float32), NEG_INF)
    _, sel_idx = lax.top_k(score, n_sel)

    n_qb = T // Q_BLOCK
    span = WINDOW + Q_BLOCK
    ks_blk = k_s.reshape(B, G, n_blk, SLC_BLOCK, dh)
    vs_blk = v_s.reshape(B, G, n_blk, SLC_BLOCK, dh)
    kw_pad = jnp.pad(k_w, ((0, 0), (0, 0), (WINDOW, 0), (0, 0)))
    vw_pad = jnp.pad(v_w, ((0, 0), (0, 0), (WINDOW, 0), (0, 0)))
    q_blocks = q.reshape(B, G, HG, n_qb, Q_BLOCK, dh).transpose(3, 0, 1, 2, 4, 5)
    idx_blocks = sel_idx.reshape(B, G, n_qb, Q_BLOCK, n_sel).transpose(2, 0, 1, 3, 4)
    gather = jax.vmap(jax.vmap(lambda kv_blk, ind: kv_blk[ind]))

    def step(args):
        c, qb, ib = args
        t = c * Q_BLOCK + jnp.arange(Q_BLOCK)
        k_sel = gather(ks_blk, ib)
        v_sel = gather(vs_blk, ib)
        pos = ib[..., None] * SLC_BLOCK + jnp.arange(SLC_BLOCK)
        msk = pos <= t[:, None, None]
        s_sel = jnp.einsum("bghqd,bgqnld->bghqnl", qb, k_sel).astype(jnp.float32) * scale
        s_sel = jnp.where(msk[:, :, None], s_sel, NEG_INF).reshape(B, G, HG, Q_BLOCK, n_sel * SLC_BLOCK)
        p_sel = jax.nn.softmax(s_sel, axis=-1)
        o_sel = jnp.einsum("bghqm,bgqmd->bghqd", p_sel.astype(v_sel.dtype),
                           v_sel.reshape(B, G, Q_BLOCK, n_sel * SLC_BLOCK, dh))
        k_win = lax.dynamic_slice_in_dim(kw_pad, c * Q_BLOCK, span, axis=2)
        v_win = lax.dynamic_slice_in_dim(vw_pad, c * Q_BLOCK, span, axis=2)
        s_pos = c * Q_BLOCK - WINDOW + jnp.arange(span)
        w_mask = ((s_pos[None] <= t[:, None]) & (s_pos[None] > t[:, None] - WINDOW)
                  & (s_pos[None] >= 0))
        s_w = jnp.einsum("bghqd,bgkd->bghqk", qb, k_win).astype(jnp.float32) * scale
        p_w = jax.nn.softmax(jnp.where(w_mask, s_w, NEG_INF), axis=-1)
        o_win = jnp.einsum("bghqk,bgkd->bghqd", p_w.astype(v_win.dtype), v_win)
        return o_sel, o_win

    o_sel, o_win = lax.map(step, (jnp.arange(n_qb), q_blocks, idx_blocks))
    o_sel = o_sel.transpose(1, 2, 3, 0, 4, 5).reshape(B, G, HG, T, dh)
    o_win = o_win.transpose(1, 2, 3, 0, 4, 5).reshape(B, G, HG, T, dh)

    g = jax.nn.sigmoid(gates).reshape(B, T, G, HG, 3).transpose(0, 2, 3, 1, 4)
    o = g[..., 0:1] * o_cmp + g[..., 1:2] * o_sel + g[..., 2:3] * o_win
    o = o.transpose(0, 3, 1, 2, 4).reshape(B, T, H, dh)
    return rmsnorm(o, out_g.reshape(H, dh)).reshape(B, T, NSA_WIDTH)


def causal_dwconv(u, w, b):
    C = u.shape[-1]
    y = lax.conv_general_dilated(u, w[:, None, :].astype(u.dtype), window_strides=(1,),
                                 padding=[(CONV_WIDTH - 1, 0)],
                                 dimension_numbers=("NWC", "WIO", "NWC"),
                                 feature_group_count=C)
    return y + b


def mlstm_mixer(q, k, v, i_pre, f_pre, o_pre, conv_w, conv_b, f_bias, out_g):
    B, T, _ = q.shape
    NH, DH, L = M_HEADS, M_HEAD_DIM, M_CHUNK
    n_ck = T // L
    f32 = jnp.float32
    qk = jax.nn.silu(causal_dwconv(jnp.concatenate([q, k], axis=-1), conv_w, conv_b))
    q, k = qk[..., :M_WIDTH], qk[..., M_WIDTH:]

    def heads(a):
        return a.astype(f32).reshape(B, n_ck, L, NH, DH).transpose(1, 0, 3, 2, 4)

    def gate(a):
        return a.astype(f32).reshape(B, n_ck, L, NH).transpose(1, 0, 3, 2)

    qh, kh, vh = heads(q), heads(k) * (DH ** -0.5), heads(v)
    log_f = gate(jax.nn.log_sigmoid(f_pre.astype(f32) + f_bias.astype(f32)))
    log_i = gate(i_pre)
    b_cum = jnp.cumsum(log_f, axis=-1)
    causal = jnp.tril(jnp.ones((L, L), dtype=bool))

    def chunk_step(carry, xs):
        C, n, m = carry
        qc, kc, vc, bc, ic = xs
        a = bc + m[..., None]
        d = jnp.where(causal, bc[..., :, None] - bc[..., None, :] + ic[..., None, :], NEG_INF)
        m_t = jnp.maximum(a, jnp.max(d, axis=-1))
        w_inter = jnp.exp(a - m_t)
        w_intra = jnp.exp(d - m_t[..., None]) * jnp.einsum("bhtd,bhsd->bhts", qc, kc)
        num = (w_inter[..., None] * jnp.einsum("bhtd,bhde->bhte", qc, C)
               + jnp.einsum("bhts,bhse->bhte", w_intra, vc))
        den = w_inter * jnp.einsum("bhtd,bhd->bht", qc, n) + jnp.sum(w_intra, axis=-1)
        h = num / jnp.maximum(jnp.abs(den), jnp.exp(-m_t))[..., None]
        b_last = bc[..., -1]
        g = b_last[..., None] - bc + ic
        m_new = jnp.maximum(b_last + m, jnp.max(g, axis=-1))
        decay = jnp.exp(b_last + m - m_new)
        w_s = jnp.exp(g - m_new[..., None])
        C_new = decay[..., None, None] * C + jnp.einsum("bhs,bhsd,bhse->bhde", w_s, kc, vc)
        n_new = decay[..., None] * n + jnp.einsum("bhs,bhsd->bhd", w_s, kc)
        return (C_new, n_new, m_new), h

    init = (jnp.zeros((B, NH, DH, DH), f32), jnp.zeros((B, NH, DH), f32), jnp.zeros((B, NH), f32))
    _, h = lax.scan(chunk_step, init, (qh, kh, vh, b_cum, log_i))
    h = h.transpose(1, 0, 3, 2, 4).reshape(B, T, NH, DH)
    h = rmsnorm(h, out_g.reshape(NH, DH)).reshape(B, T, M_WIDTH).astype(o_pre.dtype)
    return jax.nn.sigmoid(o_pre) * h


def setup_inputs(seed: int = 0) -> dict:
    key = jax.random.key(seed)
    ks = jax.random.split(key, 20)
    nrm = lambda k, shape, s: jax.random.normal(k, shape, jnp.float32) * s
    dh = NSA_HEAD_DIM
    return {
        "x": nrm(ks[0], (BATCH, SEQ, D_MODEL), 1.0),
        "ln1_g": 1.0 + nrm(ks[1], (DEPTH, D_MODEL), 0.02),
        "w_in": nrm(ks[2], (DEPTH, D_MODEL, D_IN_PROJ), D_MODEL ** -0.5),
        "b_in": nrm(ks[3], (DEPTH, D_IN_PROJ), 0.02),
        "nsa_q_norm_g": 1.0 + nrm(ks[4], (DEPTH, dh), 0.02),
        "nsa_k_norm_g": 1.0 + nrm(ks[5], (DEPTH, 3, dh), 0.02),
        "cmp_pe": nrm(ks[6], (DEPTH, 2, CMP_BLOCK, dh), 0.02),
        "cmp_w1": nrm(ks[7], (DEPTH, 2, CMP_BLOCK * dh, CMP_HIDDEN), (CMP_BLOCK * dh) ** -0.5),
        "cmp_w2": nrm(ks[8], (DEPTH, 2, CMP_HIDDEN, dh), CMP_HIDDEN ** -0.5),
        "m_conv_w": nrm(ks[9], (DEPTH, CONV_WIDTH, 2 * M_WIDTH), CONV_WIDTH ** -0.5),
        "m_conv_b": nrm(ks[10], (DEPTH, 2 * M_WIDTH), 0.02),
        "m_fgate_b": jnp.linspace(3.0, 6.0, M_HEADS, dtype=jnp.float32)[None, :] + nrm(ks[11], (DEPTH, M_HEADS), 0.1),
        "nsa_out_norm_g": 1.0 + nrm(ks[12], (DEPTH, NSA_WIDTH), 0.02),
        "m_out_norm_g": 1.0 + nrm(ks[13], (DEPTH, M_WIDTH), 0.02),
        "w_out": nrm(ks[14], (DEPTH, D_MIX, D_MODEL), D_MIX ** -0.5),
        "ln2_g": 1.0 + nrm(ks[15], (DEPTH, D_MODEL), 0.02),
        "w_gate_up": nrm(ks[16], (DEPTH, D_MODEL, 2 * FFN_HIDDEN), D_MODEL ** -0.5),
        "w_down": nrm(ks[17], (DEPTH, FFN_HIDDEN, D_MODEL), FFN_HIDDEN ** -0.5),
    }


def reference(x, ln1_g, w_in, b_in, nsa_q_norm_g, nsa_k_norm_g, cmp_pe, cmp_w1, cmp_w2,
              m_conv_w, m_conv_b, m_fgate_b, nsa_out_norm_g, m_out_norm_g, w_out,
              ln2_g, w_gate_up, w_down):
    offsets = np.cumsum(in_proj_sizes())[:-1].tolist()
    for l in range(DEPTH):
        h = rmsnorm(x, ln1_g[l])
        proj = h @ w_in[l] + b_in[l]
        (q_a, kc, vc, ksl, vsl, kwn, vwn, g_a,
         q_b, k_b, v_b, i_b, f_b, o_b) = jnp.split(proj, offsets, axis=-1)
        y_a = nsa_mixer(q_a, (kc, vc, ksl, vsl, kwn, vwn), g_a, nsa_q_norm_g[l], nsa_k_norm_g[l],
                        cmp_pe[l], cmp_w1[l], cmp_w2[l], nsa_out_norm_g[l])
        y_b = mlstm_mixer(q_b, k_b, v_b, i_b, f_b, o_b, m_conv_w[l], m_conv_b[l],
                          m_fgate_b[l], m_out_norm_g[l])
        x = x + jnp.concatenate([y_a, y_b.astype(y_a.dtype)], axis=-1) @ w_out[l]
        h = rmsnorm(x, ln2_g[l])
        gu = h @ w_gate_up[l]
        x = x + (jax.nn.silu(gu[..., :FFN_HIDDEN]) * gu[..., FFN_HIDDEN:]) @ w_down[l]
    return x
```

```python
import functools

import jax
import jax.numpy as jnp
from jax import lax
from jax.experimental import pallas as pl
from jax.experimental.pallas import tpu as pltpu

F32 = jnp.float32
BF16 = jnp.bfloat16

D_MODEL = 1024
NSA_HEADS = 8
NSA_KV_GROUPS = 2
NSA_HG = NSA_HEADS // NSA_KV_GROUPS
NSA_DH = 64
NSA_WIDTH = NSA_HEADS * NSA_DH
NSA_KV_WIDTH = NSA_KV_GROUPS * NSA_DH
CMP_BLOCK = 32
CMP_STRIDE = 16
CMP_HIDDEN = 2 * NSA_DH
SLC_BLOCK = 64
SLC_TOPN = 16
WINDOW = 512
FORCE_BONUS = 1e4
M_HEADS = 4
M_WIDTH = 512
M_DH = 128
CONV_WIDTH = 4
FFN_HIDDEN = 2816
RMS_EPS = 1e-6
NEG_INF = -1e30

LANES = 128
SUBLANES = 8
VMEM_LIMIT = 56 * 1024 * 1024

PROJ_TM = 512
ATT_TQ = 128
ATT_TK = 128
M_CHUNK = 256
FFN_FC = 256

SMALL_W = 2 * LANES
GATE_COLS = 3 * NSA_HG
I_COL = GATE_COLS
F_COL = GATE_COLS + M_HEADS


def _dot(a, b):
    return jnp.dot(a, b, preferred_element_type=F32)


def _dot_nt(a, b):
    return lax.dot_general(a, b, (((1,), (1,)), ((), ())), preferred_element_type=F32)


def _dot_tn(a, b):
    return lax.dot_general(a, b, (((0,), (0,)), ((), ())), preferred_element_type=F32)


def _split_hi_lo(a):
    hi = a.astype(BF16)
    lo = (a - hi.astype(F32)).astype(BF16)
    return hi, lo


def _silu(x):
    return x * (1.0 / (1.0 + jnp.exp(-x)))


def _sigmoid(x):
    return 1.0 / (1.0 + jnp.exp(-x))


def _log_sigmoid(x):
    return jnp.minimum(x, 0.0) - jnp.log(1.0 + jnp.exp(-jnp.abs(x)))


def _rms_rows(x, g):
    ms = jnp.mean(x * x, axis=-1, keepdims=True)
    return x * lax.rsqrt(ms + RMS_EPS) * g


def _params(*sem):
    return pltpu.CompilerParams(dimension_semantics=sem, vmem_limit_bytes=VMEM_LIMIT)


def _resident(shape):
    nd = len(shape)
    return pl.BlockSpec(shape, lambda *_: (0,) * nd, pipeline_mode=pl.Buffered(1))


IN_SEGS = (("q_a", NSA_WIDTH), ("kv", 6 * NSA_KV_WIDTH), ("qk_b", 2 * M_WIDTH),
           ("v_b", M_WIDTH), ("o_b", M_WIDTH), ("small", SMALL_W))
IN_COLS = sum(w for _, w in IN_SEGS)


def _in_proj_kernel(x_ref, g_ref, w_ref, b_ref, *out_refs):
    x = x_ref[...]
    h = _rms_rows(x, g_ref[...]).astype(BF16)
    c0 = 0
    for o_ref, (_, width) in zip(out_refs, IN_SEGS):
        o_ref[...] = _dot(h, w_ref[:, c0:c0 + width]) + b_ref[:, c0:c0 + width]
        c0 += width


def _permute_in_proj(w_in, b_in):
    o = 0
    parts = {}
    for name, width in (("q_a", NSA_WIDTH), ("kv", 6 * NSA_KV_WIDTH), ("g_a", 3 * NSA_HEADS),
                        ("q_b", M_WIDTH), ("k_b", M_WIDTH), ("v_b", M_WIDTH),
                        ("i_b", M_HEADS), ("f_b", M_HEADS), ("o_b", M_WIDTH)):
        parts[name] = (o, o + width)
        o += width

    def perm(a):
        def sl(name):
            lo, hi = parts[name]
            return a[..., lo:hi]
        lead = a.shape[:-1]
        g_a = sl("g_a")
        small0 = jnp.concatenate([g_a[..., :GATE_COLS],
                                  jnp.zeros(lead + (LANES - GATE_COLS,), a.dtype)], axis=-1)
        small1 = jnp.concatenate([g_a[..., GATE_COLS:], sl("i_b"), sl("f_b"),
                                  jnp.zeros(lead + (LANES - GATE_COLS - 2 * M_HEADS,), a.dtype)], axis=-1)
        return jnp.concatenate([sl("q_a"), sl("kv"), sl("q_b"), sl("k_b"), sl("v_b"), sl("o_b"),
                                small0, small1], axis=-1)

    return perm(w_in).astype(BF16), perm(b_in)[None, :]


def _in_proj(x2, ln_g, w_perm, b_perm):
    n = x2.shape[0]
    tm = min(PROJ_TM, n)
    out_shape = [jax.ShapeDtypeStruct((n, w), F32) for _, w in IN_SEGS]
    out_specs = [pl.BlockSpec((tm, w), lambda i: (i, 0)) for _, w in IN_SEGS]
    return pl.pallas_call(
        _in_proj_kernel,
        grid=(n // tm,),
        in_specs=[pl.BlockSpec((tm, D_MODEL), lambda i: (i, 0)),
                  _resident((1, D_MODEL)),
                  _resident((D_MODEL, IN_COLS)),
                  _resident((1, IN_COLS))],
        out_specs=out_specs,
        out_shape=out_shape,
        compiler_params=_params("parallel"),
        name="in_proj",
    )(x2, ln_g[None, :], w_perm, b_perm)


def _cmp_kernel(a_ref, pe_ref, w1_ref, w2_ref, kg_ref, o_ref):
    which = pl.program_id(0)
    a = a_ref[...]
    half = CMP_STRIDE * NSA_DH
    u = _dot((a + pe_ref[0:1, :]).astype(BF16), w1_ref[0:half, :])
    v = _dot((a + pe_ref[1:2, :]).astype(BF16), w1_ref[half:, :])
    nc = a.shape[0]
    hid = u + pltpu.roll(v, nc - 1, 0)
    out = _dot(_silu(hid).astype(BF16), w2_ref[...])
    normed = _rms_rows(out, kg_ref[...])
    o_ref[...] = jnp.where(which == 0, normed, out)


def _compress(a, pe, w1, w2, kg):
    _, b, g, nc, width = a.shape
    return pl.pallas_call(
        _cmp_kernel,
        grid=(2, b, g),
        in_specs=[pl.BlockSpec((None, None, None, nc, width), lambda w, i, j: (w, i, j, 0, 0)),
                  pl.BlockSpec((None, 2, width), lambda w, i, j: (w, 0, 0)),
                  pl.BlockSpec((None, 2 * width, CMP_HIDDEN), lambda w, i, j: (w, 0, 0)),
                  pl.BlockSpec((None, CMP_HIDDEN, NSA_DH), lambda w, i, j: (w, 0, 0)),
                  pl.BlockSpec((1, NSA_DH), lambda w, i, j: (0, 0))],
        out_specs=pl.BlockSpec((None, None, None, nc, NSA_DH), lambda w, i, j: (w, i, j, 0, 0)),
        out_shape=jax.ShapeDtypeStruct((2, b, g, nc, NSA_DH), F32),
        compiler_params=_params("parallel", "parallel", "parallel"),
        name="cmp_kv",
    )(a, pe, w1, w2, kg)


def _knorm_kernel(k_ref, g_ref, o_ref):
    o_ref[...] = _rms_rows(k_ref[...], g_ref[...]).astype(BF16)


def _knorm(k, g):
    _, r, dh = k.shape
    tr = min(4096, r)
    return pl.pallas_call(
        _knorm_kernel,
        grid=(2, r // tr),
        in_specs=[pl.BlockSpec((None, tr, dh), lambda w, i: (w, i, 0)),
                  pl.BlockSpec((None, 1, dh), lambda w, i: (w, 0, 0))],
        out_specs=pl.BlockSpec((None, tr, dh), lambda w, i: (w, i, 0)),
        out_shape=jax.ShapeDtypeStruct((2, r, dh), BF16),
        compiler_params=_params("parallel", "parallel"),
        name="k_norm",
    )(k, g)


def _online_update(s, v, m_ref, l_ref, acc_ref):
    m_prev = m_ref[...]
    m_new = jnp.maximum(m_prev, jnp.max(s, axis=-1, keepdims=True))
    alpha = jnp.exp(m_prev - m_new)
    p = jnp.exp(s - m_new)
    l_ref[...] = alpha * l_ref[...] + jnp.sum(p, axis=-1, keepdims=True)
    acc_ref[...] = alpha * acc_ref[...] + _dot(p.astype(BF16), v)
    m_ref[...] = m_new


def _nsa_kernel(q_ref, kc_ref, vc_ref, ks_ref, vs_ref, kw_ref, vw_ref, gate_ref, qg_ref, og_ref,
                o_ref, m_s, l_s, acc_s, m_w, l_w, acc_w, *, seq_len):
    tq, tk, hg, dh = ATT_TQ, ATT_TK, NSA_HG, NSA_DH
    rows = hg * tq
    n_cmp = seq_len // CMP_STRIDE
    n_blk = seq_len // SLC_BLOCK
    n_sel = min(SLC_TOPN, n_blk)
    i = pl.program_id(2)
    t0 = i * tq

    q = q_ref[...]
    q4 = jnp.concatenate([q[:, h * dh:(h + 1) * dh] for h in range(hg)], axis=0)
    qn = (_rms_rows(q4, qg_ref[...]) * (dh ** -0.5)).astype(BF16)

    s_c = _dot_nt(qn, kc_ref[...].astype(BF16))
    t_row = t0 + lax.broadcasted_iota(jnp.int32, (tq, n_cmp), 0)
    c_end = lax.broadcasted_iota(jnp.int32, (tq, n_cmp), 1) * CMP_STRIDE + (CMP_BLOCK - 1)
    c_valid = jnp.concatenate([c_end <= t_row] * hg, axis=0)
    s_c = jnp.where(c_valid, s_c, NEG_INF)
    m_c = jnp.max(s_c, axis=-1, keepdims=True)
    e_c = jnp.where(c_valid, jnp.exp(s_c - m_c), 0.0)
    l_c = jnp.sum(e_c, axis=-1, keepdims=True)
    p_c = e_c * (1.0 / jnp.where(l_c > 0.0, l_c, 1.0))
    o_cmp = _dot(p_c.astype(BF16), vc_ref[...].astype(BF16))

    p_sum = p_c[0:tq]
    for h in range(1, hg):
        p_sum = p_sum + p_c[h * tq:(h + 1) * tq]
    blk_r = lax.broadcasted_iota(jnp.int32, (n_blk, n_cmp), 0)
    cmp_c = lax.broadcasted_iota(jnp.int32, (n_blk, n_cmp), 1)
    overlap_t = ((cmp_c * CMP_STRIDE < (blk_r + 1) * SLC_BLOCK)
                 & (cmp_c * CMP_STRIDE + (CMP_BLOCK - 1) >= blk_r * SLC_BLOCK)
                 & (cmp_c < n_cmp - 1))
    overlap_t = jnp.where(overlap_t, 1.0, 0.0).astype(BF16)
    p_hi, p_lo = _split_hi_lo(p_sum)
    imp_t = _dot_nt(overlap_t, p_hi) + _dot_nt(overlap_t, p_lo)

    blk = lax.broadcasted_iota(jnp.int32, (n_blk, tq), 0)
    t_col = t0 + lax.broadcasted_iota(jnp.int32, (n_blk, tq), 1)
    cur = t_col // SLC_BLOCK
    forced = (blk == 0) | (blk == cur) | (blk == cur - 1)
    score = jnp.where(blk * SLC_BLOCK <= t_col,
                      imp_t + jnp.where(forced, FORCE_BONUS, 0.0), NEG_INF)
    rank = jnp.zeros((n_blk, tq), F32)
    for jp in range(n_blk):
        row = score[jp:jp + 1, :]
        rank = rank + jnp.where(blk > jp, jnp.where(row >= score, 1.0, 0.0),
                                jnp.where(row > score, 1.0, 0.0))
    neg_t = jnp.where(rank < float(n_sel), 0.0, NEG_INF)
    if n_blk < tk:
        neg_t = jnp.concatenate([neg_t, jnp.full((tk - n_blk, tq), NEG_INF, F32)], axis=0)
    neg_mask = neg_t.T.astype(BF16)

    m_s[...] = jnp.full((rows, 1), NEG_INF, F32)
    l_s[...] = jnp.zeros((rows, 1), F32)
    acc_s[...] = jnp.zeros((rows, dh), F32)
    blk_per_tile = tk // SLC_BLOCK
    e_row = lax.broadcasted_iota(jnp.int32, (neg_mask.shape[1], tk), 0)
    e_blk = lax.broadcasted_iota(jnp.int32, (neg_mask.shape[1], tk), 1) // SLC_BLOCK
    q_pos = t0 + lax.broadcasted_iota(jnp.int32, (tq, tk), 0)
    k_off = lax.broadcasted_iota(jnp.int32, (tq, tk), 1)

    def sel_body(kt, carry):
        k0 = pl.multiple_of(kt * tk, tk)
        k = ks_ref[pl.ds(k0, tk), :]
        v = vs_ref[pl.ds(k0, tk), :]
        expand = jnp.where(e_row == kt * blk_per_tile + e_blk, 1.0, 0.0).astype(BF16)
        bias = _dot(neg_mask, expand)
        bias = jnp.where(k0 + k_off <= q_pos, bias, NEG_INF)
        s = _dot_nt(qn, k) + jnp.concatenate([bias] * hg, axis=0)
        _online_update(s, v, m_s, l_s, acc_s)
        return carry

    lax.fori_loop(0, (t0 + tq) // tk, sel_body, 0)

    m_w[...] = jnp.full((rows, 1), NEG_INF, F32)
    l_w[...] = jnp.zeros((rows, 1), F32)
    acc_w[...] = jnp.zeros((rows, dh), F32)
    n_win_tiles = (WINDOW + tq) // tk
    for d in range(n_win_tiles):
        k_start = t0 + tq - (n_win_tiles - d) * tk

        @pl.when(k_start >= 0)
        def _():
            k0 = pl.multiple_of(jnp.maximum(k_start, 0), tk)
            k = kw_ref[pl.ds(k0, tk), :]
            v = vw_ref[pl.ds(k0, tk), :]
            k_pos = k_start + k_off
            ok = (k_pos <= q_pos) & (k_pos > q_pos - WINDOW)
            bias = jnp.where(ok, 0.0, NEG_INF)
            s = _dot_nt(qn, k) + jnp.concatenate([bias] * hg, axis=0)
            _online_update(s, v, m_w, l_w, acc_w)

    o_sel = acc_s[...] * (1.0 / l_s[...])
    o_win = acc_w[...] * (1.0 / l_w[...])

    gate = _sigmoid(gate_ref[...])
    outs = []
    for h in range(hg):
        r = slice(h * tq, (h + 1) * tq)
        o = (gate[:, 3 * h:3 * h + 1] * o_cmp[r] + gate[:, 3 * h + 1:3 * h + 2] * o_sel[r]
             + gate[:, 3 * h + 2:3 * h + 3] * o_win[r])
        outs.append(_rms_rows(o, og_ref[:, h * dh:(h + 1) * dh]))
    o_ref[...] = jnp.concatenate(outs, axis=-1)


def _nsa_attn(q_a, k_cmp, v_cmp, ks, vs, kw, vw, small, q_g, out_g, batch, seq_len):
    n = q_a.shape[0]
    g, hg, dh, tq = NSA_KV_GROUPS, NSA_HG, NSA_DH, ATT_TQ
    nq = seq_len // tq
    n_cmp = seq_len // CMP_STRIDE
    rows = hg * tq
    kv_full = pl.BlockSpec((None, None, seq_len, dh), lambda b, j, i: (b, j, 0, 0))
    cmp_full = pl.BlockSpec((None, None, n_cmp, dh), lambda b, j, i: (b, j, 0, 0))
    return pl.pallas_call(
        functools.partial(_nsa_kernel, seq_len=seq_len),
        grid=(batch, g, nq),
        in_specs=[pl.BlockSpec((tq, hg * dh), lambda b, j, i: (b * nq + i, j)),
                  cmp_full, cmp_full, kv_full, kv_full, kv_full, kv_full,
                  pl.BlockSpec((tq, LANES), lambda b, j, i: (b * nq + i, j)),
                  pl.BlockSpec((1, dh), lambda b, j, i: (0, 0)),
                  pl.BlockSpec((1, hg * dh), lambda b, j, i: (0, j))],
        out_specs=pl.BlockSpec((tq, hg * dh), lambda b, j, i: (b * nq + i, j)),
        out_shape=jax.ShapeDtypeStruct((n, NSA_WIDTH), F32),
        scratch_shapes=[pltpu.VMEM((rows, 1), F32), pltpu.VMEM((rows, 1), F32),
                        pltpu.VMEM((rows, dh), F32),
                        pltpu.VMEM((rows, 1), F32), pltpu.VMEM((rows, 1), F32),
                        pltpu.VMEM((rows, dh), F32)],
        compiler_params=_params("parallel", "parallel", "arbitrary"),
        name="nsa_attn",
    )(q_a, k_cmp, v_cmp, ks, vs, kw, vw, small, q_g, out_g)


def _mlstm_kernel(qk_ref, v_ref, o_ref, small_ref, ift_ref, cw_ref, cb_ref, fb_ref, fbt_ref, og_ref,
                  y_ref, xbuf, c_st, n_st, m_st):
    lc = M_CHUNK
    nh, dh, width = M_HEADS, M_DH, M_WIDTH
    c = pl.program_id(1)

    @pl.when(c == 0)
    def _():
        xbuf[0:SUBLANES, :] = jnp.zeros((SUBLANES, 2 * width), F32)
        c_st[...] = jnp.zeros(c_st.shape, F32)
        n_st[...] = jnp.zeros(n_st.shape, F32)
        m_st[...] = jnp.zeros(m_st.shape, F32)

    u = qk_ref[...]
    xbuf[SUBLANES:SUBLANES + lc, :] = u
    y = cb_ref[...] + cw_ref[0:1, :] * xbuf[pl.ds(SUBLANES - (CONV_WIDTH - 1), lc), :]
    for j in range(1, CONV_WIDTH):
        y = y + cw_ref[j:j + 1, :] * xbuf[pl.ds(SUBLANES - (CONV_WIDTH - 1) + j, lc), :]
    xbuf[0:SUBLANES, :] = u[lc - SUBLANES:lc, :]
    qk = _silu(y)

    r_i = lax.broadcasted_iota(jnp.int32, (lc, lc), 0)
    c_i = lax.broadcasted_iota(jnp.int32, (lc, lc), 1)
    causal = c_i <= r_i
    tril = jnp.where(causal, 1.0, 0.0).astype(BF16)
    triu = jnp.where(r_i <= c_i, 1.0, 0.0).astype(BF16)
    small = small_ref[...]
    lf_hi, lf_lo = _split_hi_lo(_log_sigmoid(small + fb_ref[...]))
    b_col_all = _dot(tril, lf_hi) + _dot(tril, lf_lo)
    ift = ift_ref[...]
    lft_hi, lft_lo = _split_hi_lo(_log_sigmoid(ift + fbt_ref[:, 0:1]))
    b_row_all = _dot(lft_hi, triu) + _dot(lft_lo, triu)

    v_all = v_ref[...]
    og_all = _sigmoid(o_ref[...])
    for h in range(nh):
        hs = slice(h * dh, (h + 1) * dh)
        q_h = qk[:, h * dh:(h + 1) * dh]
        k_h = qk[:, width + h * dh:width + (h + 1) * dh] * (dh ** -0.5)
        v_h = v_all[:, hs]
        q_b, k_b, v_b = q_h.astype(BF16), k_h.astype(BF16), v_h.astype(BF16)
        b_col = b_col_all[:, F_COL + h:F_COL + h + 1]
        i_col = small[:, I_COL + h:I_COL + h + 1]
        b_row = b_row_all[nh + h:nh + h + 1, :]
        i_row = ift[h:h + 1, :]
        m_prev = m_st[h:h + 1, 0:1]
        c_prev = c_st[h]
        n_prev = n_st[h:h + 1, :]

        a = b_col + m_prev
        d = jnp.where(causal, b_col + (i_row - b_row), NEG_INF)
        m_t = jnp.maximum(a, jnp.max(d, axis=-1, keepdims=True))
        w_inter = jnp.exp(a - m_t)
        w_intra = jnp.exp(d - m_t) * _dot_nt(q_b, k_b)
        num = w_inter * _dot(q_b, c_prev.astype(BF16)) + _dot(w_intra.astype(BF16), v_b)
        den = (w_inter * jnp.sum(q_h * n_prev, axis=-1, keepdims=True)
               + jnp.sum(w_intra, axis=-1, keepdims=True))
        hid = num * (1.0 / jnp.maximum(jnp.abs(den), jnp.exp(-m_t)))

        b_last = b_col[lc - 1:lc, :]
        g_col = b_last - b_col + i_col
        m_new = jnp.maximum(b_last + m_prev, jnp.max(g_col, axis=0, keepdims=True))
        decay = jnp.exp(b_last + m_prev - m_new)
        w_s = jnp.exp(g_col - m_new)
        c_st[h] = decay * c_prev + _dot_tn(k_b, (w_s * v_h).astype(BF16))
        n_st[h:h + 1, :] = decay * n_prev + jnp.sum(w_s * k_h, axis=0, keepdims=True)
        m_st[h:h + 1, :] = jnp.broadcast_to(m_new, (1, LANES))

        y_ref[:, hs] = og_all[:, hs] * _rms_rows(hid, og_ref[:, hs])


def _mlstm(qk_b, v_b, o_b, small, if_t, conv_w, conv_b, fb_col, fb_row, out_g, batch, seq_len):
    n = qk_b.shape[0]
    lc = M_CHUNK
    nc = seq_len // lc
    row = lambda b, c: (b * nc + c, 0)
    return pl.pallas_call(
        _mlstm_kernel,
        grid=(batch, nc),
        in_specs=[pl.BlockSpec((lc, 2 * M_WIDTH), row),
                  pl.BlockSpec((lc, M_WIDTH), row),
                  pl.BlockSpec((lc, M_WIDTH), row),
                  pl.BlockSpec((lc, LANES), lambda b, c: (b * nc + c, 1)),
                  pl.BlockSpec((None, 2 * M_HEADS, lc), lambda b, c: (b, 0, c)),
                  pl.BlockSpec((CONV_WIDTH, 2 * M_WIDTH), lambda b, c: (0, 0)),
                  pl.BlockSpec((1, 2 * M_WIDTH), lambda b, c: (0, 0)),
                  pl.BlockSpec((1, LANES), lambda b, c: (0, 0)),
                  pl.BlockSpec((2 * M_HEADS, LANES), lambda b, c: (0, 0)),
                  pl.BlockSpec((1, M_WIDTH), lambda b, c: (0, 0))],
        out_specs=pl.BlockSpec((lc, M_WIDTH), row),
        out_shape=jax.ShapeDtypeStruct((n, M_WIDTH), F32),
        scratch_shapes=[pltpu.VMEM((lc + SUBLANES, 2 * M_WIDTH), F32),
                        pltpu.VMEM((M_HEADS, M_DH, M_DH), F32),
                        pltpu.VMEM((SUBLANES, M_DH), F32),
                        pltpu.VMEM((SUBLANES, LANES), F32)],
        compiler_params=_params("parallel", "arbitrary"),
        name="mlstm",
    )(qk_b, v_b, o_b, small, if_t, conv_w, conv_b, fb_col, fb_row, out_g)


def _out_ffn_kernel(x_ref, ya_ref, yb_ref, wo_ref, g_ref, wgu_ref, wd_ref, o_ref):
    x1 = (x_ref[...] + _dot(ya_ref[...].astype(BF16), wo_ref[0:NSA_WIDTH, :])
          + _dot(yb_ref[...].astype(BF16), wo_ref[NSA_WIDTH:, :]))
    h = _rms_rows(x1, g_ref[...]).astype(BF16)
    o_ref[...] = x1
    for c in range(FFN_HIDDEN // FFN_FC):
        cs = slice(c * FFN_FC, (c + 1) * FFN_FC)
        us = slice(FFN_HIDDEN + c * FFN_FC, FFN_HIDDEN + (c + 1) * FFN_FC)
        act = (_silu(_dot(h, wgu_ref[:, cs])) * _dot(h, wgu_ref[:, us])).astype(BF16)
        o_ref[...] += _dot(act, wd_ref[cs, :])


def _out_ffn(x2, y_a, y_b, w_out, ln_g, w_gu, w_down):
    n = x2.shape[0]
    tm = min(PROJ_TM, n)
    row = lambda i: (i, 0)
    return pl.pallas_call(
        _out_ffn_kernel,
        grid=(n // tm,),
        in_specs=[pl.BlockSpec((tm, D_MODEL), row),
                  pl.BlockSpec((tm, NSA_WIDTH), row),
                  pl.BlockSpec((tm, M_WIDTH), row),
                  _resident((D_MODEL, D_MODEL)),
                  _resident((1, D_MODEL)),
                  _resident((D_MODEL, 2 * FFN_HIDDEN)),
                  _resident((FFN_HIDDEN, D_MODEL))],
        out_specs=pl.BlockSpec((tm, D_MODEL), row),
        out_shape=jax.ShapeDtypeStruct((n, D_MODEL), F32),
        compiler_params=_params("parallel"),
        name="out_ffn",
    )(x2, y_a, y_b, w_out, ln_g[None, :], w_gu, w_down)


def _layer(x2, batch, seq_len, ln1_g, w_in, b_in, q_g, k_g, pe, w1, w2, conv_w, conv_b, fgate_b,
           nsa_out_g, m_out_g, w_out, ln2_g, w_gu, w_down):
    g, dh = NSA_KV_GROUPS, NSA_DH
    w_perm, b_perm = _permute_in_proj(w_in, b_in)
    q_a, kv, qk_b, v_b, o_b, small = _in_proj(x2, ln1_g, w_perm, b_perm)

    kv6 = kv.reshape(batch, seq_len, 6, g, dh).transpose(2, 0, 3, 1, 4)
    n_cmp = seq_len // CMP_STRIDE
    cmp_in = kv6[0:2].reshape(2, batch, g, n_cmp, CMP_STRIDE * dh)
    cmp_out = _compress(cmp_in, pe.reshape(2, 2, CMP_STRIDE * dh), w1.astype(BF16), w2.astype(BF16),
                        k_g[0:1])
    k_normed = _knorm(jnp.stack([kv6[2], kv6[4]]).reshape(2, batch * g * seq_len, dh),
                      k_g[1:3, None, :]).reshape(2, batch, g, seq_len, dh)
    vs = kv6[3].astype(BF16)
    vw = kv6[5].astype(BF16)
    y_a = _nsa_attn(q_a, cmp_out[0], cmp_out[1], k_normed[0], vs, k_normed[1], vw, small,
                    q_g[None, :], nsa_out_g[None, :], batch, seq_len)

    if_t = small[:, LANES + I_COL:LANES + I_COL + 2 * M_HEADS].reshape(
        batch, seq_len, 2 * M_HEADS).transpose(0, 2, 1)
    fb_col = jnp.zeros((1, LANES), F32).at[0, F_COL:F_COL + M_HEADS].set(fgate_b)
    fb_row = jnp.zeros((2 * M_HEADS, LANES), F32).at[M_HEADS:, :].set(
        jnp.broadcast_to(fgate_b[:, None], (M_HEADS, LANES)))
    y_b = _mlstm(qk_b, v_b, o_b, small, if_t, conv_w, conv_b[None, :], fb_col, fb_row,
                 m_out_g[None, :], batch, seq_len)

    return _out_ffn(x2, y_a, y_b, w_out.astype(BF16), ln2_g, w_gu.astype(BF16), w_down.astype(BF16))


def kernel(x, ln1_g, w_in, b_in, nsa_q_norm_g, nsa_k_norm_g, cmp_pe, cmp_w1, cmp_w2, m_conv_w,
           m_conv_b, m_fgate_b, nsa_out_norm_g, m_out_norm_g, w_out, ln2_g, w_gate_up, w_down):
    batch, seq_len, d = x.shape
    x2 = x.reshape(batch * seq_len, d)
    for l in range(ln1_g.shape[0]):
        x2 = _layer(x2, batch, seq_len, ln1_g[l], w_in[l], b_in[l], nsa_q_norm_g[l], nsa_k_norm_g[l],
                    cmp_pe[l], cmp_w1[l], cmp_w2[l], m_conv_w[l], m_conv_b[l], m_fgate_b[l],
                    nsa_out_norm_g[l], m_out_norm_g[l], w_out[l], ln2_g[l], w_gate_up[l], w_down[l])
    return x2.reshape(batch, seq_len, d)
```

```python
import functools

import jax
import jax.numpy as jnp
from jax import lax
from jax.experimental import pallas as pl
from jax.experimental.pallas import tpu as pltpu

F32 = jnp.float32
BF16 = jnp.bfloat16

D_MODEL = 1024
NSA_HEADS = 8
NSA_KV_GROUPS = 2
NSA_HG = NSA_HEADS // NSA_KV_GROUPS
NSA_DH = 64
NSA_WIDTH = NSA_HEADS * NSA_DH
NSA_KV_WIDTH = NSA_KV_GROUPS * NSA_DH
CMP_BLOCK = 32
CMP_STRIDE = 16
CMP_HIDDEN = 2 * NSA_DH
SLC_BLOCK = 64
SLC_TOPN = 16
WINDOW = 512
FORCE_BONUS = 1e4
M_HEADS = 4
M_WIDTH = 512
M_DH = 128
CONV_WIDTH = 4
FFN_HIDDEN = 2816
RMS_EPS = 1e-6
NEG_INF = -1e30
LOG2E = 1.4426950408889634

LANES = 128
SUBLANES = 8
VMEM_LIMIT = 56 * 1024 * 1024

PROJ_TM = 512
ATT_TQ = 128
SEL_BODY = 512
SEL_SUB = 256
M_CHUNK = 256
FFN_FC = 256

SMALL_W = 2 * LANES
GATE_COLS = 3 * NSA_HG
I_COL = GATE_COLS
F_COL = GATE_COLS + M_HEADS


def _dot(a, b):
    return jnp.dot(a, b, preferred_element_type=F32)


def _dot_nt(a, b):
    return lax.dot_general(a, b, (((1,), (1,)), ((), ())), preferred_element_type=F32)


def _dot_tn(a, b):
    return lax.dot_general(a, b, (((0,), (0,)), ((), ())), preferred_element_type=F32)


def _split_hi_lo(a):
    hi = a.astype(BF16)
    lo = (a - hi.astype(F32)).astype(BF16)
    return hi, lo


def _silu(x):
    return x * (1.0 / (1.0 + jnp.exp(-x)))


def _sigmoid(x):
    return 1.0 / (1.0 + jnp.exp(-x))


def _log_sigmoid(x):
    return jnp.minimum(x, 0.0) - jnp.log(1.0 + jnp.exp(-jnp.abs(x)))


def _rms_rows(x, g):
    ms = jnp.mean(x * x, axis=-1, keepdims=True)
    return x * lax.rsqrt(ms + RMS_EPS) * g


def _params(*sem):
    return pltpu.CompilerParams(dimension_semantics=sem, vmem_limit_bytes=VMEM_LIMIT)


def _resident(shape):
    nd = len(shape)
    return pl.BlockSpec(shape, lambda *_: (0,) * nd, pipeline_mode=pl.Buffered(1))


IN_SEGS = (("q_a", NSA_WIDTH), ("kv", 6 * NSA_KV_WIDTH), ("qk_b", 2 * M_WIDTH),
           ("v_b", M_WIDTH), ("o_b", M_WIDTH), ("small", SMALL_W))
IN_COLS = sum(w for _, w in IN_SEGS)


def _in_proj_kernel(x_ref, g_ref, w_ref, b_ref, *out_refs):
    x = x_ref[...]
    h = _rms_rows(x, g_ref[...]).astype(BF16)
    c0 = 0
    for o_ref, (_, width) in zip(out_refs, IN_SEGS):
        o_ref[...] = _dot(h, w_ref[:, c0:c0 + width]) + b_ref[:, c0:c0 + width]
        c0 += width


def _permute_in_proj(w_in, b_in):
    o = 0
    parts = {}
    for name, width in (("q_a", NSA_WIDTH), ("kv", 6 * NSA_KV_WIDTH), ("g_a", 3 * NSA_HEADS),
                        ("q_b", M_WIDTH), ("k_b", M_WIDTH), ("v_b", M_WIDTH),
                        ("i_b", M_HEADS), ("f_b", M_HEADS), ("o_b", M_WIDTH)):
        parts[name] = (o, o + width)
        o += width

    def perm(a):
        def sl(name):
            lo, hi = parts[name]
            return a[..., lo:hi]
        lead = a.shape[:-1]
        g_a = sl("g_a")
        small0 = jnp.concatenate([g_a[..., :GATE_COLS],
                                  jnp.zeros(lead + (LANES - GATE_COLS,), a.dtype)], axis=-1)
        small1 = jnp.concatenate([g_a[..., GATE_COLS:], sl("i_b"), sl("f_b"),
                                  jnp.zeros(lead + (LANES - GATE_COLS - 2 * M_HEADS,), a.dtype)], axis=-1)
        return jnp.concatenate([sl("q_a"), sl("kv"), sl("q_b"), sl("k_b"), sl("v_b"), sl("o_b"),
                                small0, small1], axis=-1)

    return perm(w_in).astype(BF16), perm(b_in)[None, :]


def _in_proj(x2, ln_g, w_perm, b_perm):
    n = x2.shape[0]
    tm = min(PROJ_TM, n)
    out_shape = [jax.ShapeDtypeStruct((n, w), F32) for _, w in IN_SEGS]
    out_specs = [pl.BlockSpec((tm, w), lambda i: (i, 0)) for _, w in IN_SEGS]
    return pl.pallas_call(
        _in_proj_kernel,
        grid=(n // tm,),
        in_specs=[pl.BlockSpec((tm, D_MODEL), lambda i: (i, 0)),
                  _resident((1, D_MODEL)),
                  _resident((D_MODEL, IN_COLS)),
                  _resident((1, IN_COLS))],
        out_specs=out_specs,
        out_shape=out_shape,
        compiler_params=_params("parallel"),
        name="in_proj",
    )(x2, ln_g[None, :], w_perm, b_perm)


def _cmp_kernel(a_ref, pe_ref, w1_ref, w2_ref, kg_ref, o_ref):
    which = pl.program_id(0)
    a = a_ref[...]
    half = CMP_STRIDE * NSA_DH
    u = _dot((a + pe_ref[0:1, :]).astype(BF16), w1_ref[0:half, :])
    v = _dot((a + pe_ref[1:2, :]).astype(BF16), w1_ref[half:, :])
    nc = a.shape[0]
    hid = u + pltpu.roll(v, nc - 1, 0)
    out = _dot(_silu(hid).astype(BF16), w2_ref[...])
    normed = _rms_rows(out, kg_ref[...])
    o_ref[...] = jnp.where(which == 0, normed, out)


def _compress(a, pe, w1, w2, kg):
    _, b, g, nc, width = a.shape
    return pl.pallas_call(
        _cmp_kernel,
        grid=(2, b, g),
        in_specs=[pl.BlockSpec((None, None, None, nc, width), lambda w, i, j: (w, i, j, 0, 0)),
                  pl.BlockSpec((None, 2, width), lambda w, i, j: (w, 0, 0)),
                  pl.BlockSpec((None, 2 * width, CMP_HIDDEN), lambda w, i, j: (w, 0, 0)),
                  pl.BlockSpec((None, CMP_HIDDEN, NSA_DH), lambda w, i, j: (w, 0, 0)),
                  pl.BlockSpec((1, NSA_DH), lambda w, i, j: (0, 0))],
        out_specs=pl.BlockSpec((None, None, None, nc, NSA_DH), lambda w, i, j: (w, i, j, 0, 0)),
        out_shape=jax.ShapeDtypeStruct((2, b, g, nc, NSA_DH), F32),
        compiler_params=_params("parallel", "parallel", "parallel"),
        name="cmp_kv",
    )(a, pe, w1, w2, kg)


def _knorm_kernel(k_ref, g_ref, o_ref):
    o_ref[...] = _rms_rows(k_ref[...], g_ref[...]).astype(BF16)


def _knorm(k, g):
    _, r, dh = k.shape
    tr = min(4096, r)
    return pl.pallas_call(
        _knorm_kernel,
        grid=(2, r // tr),
        in_specs=[pl.BlockSpec((None, tr, dh), lambda w, i: (w, i, 0)),
                  pl.BlockSpec((None, 1, dh), lambda w, i: (w, 0, 0))],
        out_specs=pl.BlockSpec((None, tr, dh), lambda w, i: (w, i, 0)),
        out_shape=jax.ShapeDtypeStruct((2, r, dh), BF16),
        compiler_params=_params("parallel", "parallel"),
        name="k_norm",
    )(k, g)


def _softmax_pv(s, v_t):
    m = jnp.max(s, axis=0, keepdims=True)
    p = jnp.exp2(s - m)
    l = jnp.sum(p, axis=0, keepdims=True)
    return _dot(v_t, p.astype(BF16)) * (1.0 / l)


def _nsa_kernel(q_ref, kc_ref, vc_ref, ks_ref, vst_ref, kw_ref, vwt_ref, gate_ref, qg_ref, og_ref,
                o_ref, score_ref, neg_ref, s_ref, acc_ref, *, seq_len):
    tq, hg, dh = ATT_TQ, NSA_HG, NSA_DH
    rows = hg * tq
    n_cmp = seq_len // CMP_STRIDE
    n_blk = seq_len // SLC_BLOCK
    n_sel = min(SLC_TOPN, n_blk)
    i = pl.program_id(2)
    t0 = pl.multiple_of(i * tq, tq)

    q = q_ref[...]
    q4 = jnp.concatenate([q[:, h * dh:(h + 1) * dh] for h in range(hg)], axis=0)
    qn = (_rms_rows(q4, qg_ref[...]) * (dh ** -0.5 * LOG2E)).astype(BF16)

    def lane_tile(a):
        return jnp.concatenate([a] * hg, axis=1)

    c_end = lax.broadcasted_iota(jnp.int32, (n_cmp, tq), 0) * CMP_STRIDE + (CMP_BLOCK - 1)
    c_ok = c_end <= t0 + lax.broadcasted_iota(jnp.int32, (n_cmp, tq), 1)
    s_c = _dot_nt(kc_ref[...].astype(BF16), qn) + lane_tile(jnp.where(c_ok, 0.0, NEG_INF))
    m_c = jnp.max(s_c, axis=0, keepdims=True)
    e_c = jnp.exp2(s_c - m_c) * lane_tile(jnp.where(c_ok, 1.0, 0.0))
    l_c = jnp.sum(e_c, axis=0, keepdims=True)
    p_c = e_c * (1.0 / jnp.where(l_c > 0.0, l_c, 1.0))
    o_cmp = _dot_tn(vc_ref[...].astype(BF16), p_c.astype(BF16))

    p_sum = p_c[:, 0:tq]
    for h in range(1, hg):
        p_sum = p_sum + p_c[:, h * tq:(h + 1) * tq]
    blk_r = lax.broadcasted_iota(jnp.int32, (n_blk, n_cmp), 0)
    cmp_c = lax.broadcasted_iota(jnp.int32, (n_blk, n_cmp), 1)
    overlap = ((cmp_c * CMP_STRIDE < (blk_r + 1) * SLC_BLOCK)
               & (cmp_c * CMP_STRIDE + (CMP_BLOCK - 1) >= blk_r * SLC_BLOCK)
               & (cmp_c < n_cmp - 1))
    overlap = jnp.where(overlap, 1.0, 0.0).astype(BF16)
    p_hi, p_lo = _split_hi_lo(p_sum)
    imp = _dot(overlap, p_hi) + _dot(overlap, p_lo)

    blk = lax.broadcasted_iota(jnp.int32, (n_blk, tq), 0)
    t_col = t0 + lax.broadcasted_iota(jnp.int32, (n_blk, tq), 1)
    cur = t_col // SLC_BLOCK
    forced = (blk == 0) | (blk == cur) | (blk == cur - 1)
    score = jnp.where(blk * SLC_BLOCK <= t_col,
                      imp + jnp.where(forced, FORCE_BONUS, 0.0), NEG_INF)
    score_ref[...] = score

    def rank_body(jp, rank):
        row = score_ref[pl.ds(jp, 1), :]
        return rank + jnp.where(blk > jp, jnp.where(row >= score, 1.0, 0.0),
                                jnp.where(row > score, 1.0, 0.0))

    n_live = jnp.minimum((t0 + tq) // SLC_BLOCK, n_blk)
    rank = lax.fori_loop(0, n_live, rank_body, jnp.zeros((n_blk, tq), F32))
    neg_ref[...] = jnp.where(rank < float(n_sel), 0.0, NEG_INF)

    def v_tiles(vt_ref, k0, tk):
        return jnp.concatenate([vt_ref[k0 // LANES + jt] for jt in range(tk // LANES)], axis=1)

    def fold8(a):
        return a.reshape(a.shape[0] // SUBLANES, SUBLANES, rows)

    body, sub = SEL_BODY, SEL_SUB
    k_sub = lax.broadcasted_iota(jnp.int32, (sub, tq), 0)
    q_sub = t0 + lax.broadcasted_iota(jnp.int32, (sub, tq), 1)

    def scores_body(k0, mx, causal):
        for j in range(body // sub):
            ks0 = k0 + j * sub
            bias = jnp.concatenate(
                [jnp.broadcast_to(neg_ref[pl.ds(ks0 // SLC_BLOCK + jb, 1), :], (SLC_BLOCK, tq))
                 for jb in range(sub // SLC_BLOCK)], axis=0)
            if causal:
                bias = jnp.where(ks0 + k_sub <= q_sub, bias, NEG_INF)
            s = _dot_nt(ks_ref[pl.ds(ks0, sub), :], qn) + lane_tile(bias)
            s_ref[pl.ds(ks0, sub), :] = s
            mx = jnp.maximum(mx, jnp.max(fold8(s), axis=0))
        return mx

    def pv_body(k0, m, ls):
        p = jnp.exp2(s_ref[pl.ds(k0, body), :] - m)
        acc_ref[...] += _dot(v_tiles(vst_ref, k0, body), p.astype(BF16))
        return ls + jnp.sum(fold8(p), axis=0)

    n_full = t0 // body
    k_last = pl.multiple_of(n_full * body, body)
    mx = lax.fori_loop(0, n_full, lambda kt, mx: scores_body(pl.multiple_of(kt * body, body), mx, False),
                       jnp.full((SUBLANES, rows), NEG_INF, F32))
    mx = scores_body(k_last, mx, True)
    m_sel = jnp.max(mx, axis=0, keepdims=True)
    acc_ref[...] = jnp.zeros((dh, rows), F32)
    ls = lax.fori_loop(0, n_full, lambda kt, ls: pv_body(pl.multiple_of(kt * body, body), m_sel, ls),
                       jnp.zeros((SUBLANES, rows), F32))
    ls = pv_body(k_last, m_sel, ls)
    o_sel = acc_ref[...] * (1.0 / jnp.sum(ls, axis=0, keepdims=True))

    span = WINDOW + tq
    w0 = pl.multiple_of(jnp.clip(t0 - WINDOW, 0, seq_len - span), tq)
    k_pos = w0 + lax.broadcasted_iota(jnp.int32, (span, tq), 0)
    q_pos = t0 + lax.broadcasted_iota(jnp.int32, (span, tq), 1)
    w_bias = jnp.where(k_pos <= q_pos, jnp.where(k_pos > q_pos - WINDOW, 0.0, NEG_INF), NEG_INF)
    s_w = _dot_nt(kw_ref[pl.ds(w0, span), :], qn) + lane_tile(w_bias)
    o_win = _softmax_pv(s_w, v_tiles(vwt_ref, w0, span))

    gate_t = _sigmoid(gate_ref[...]).T
    outs = []
    for h in range(hg):
        hs = slice(h * tq, (h + 1) * tq)
        o = (gate_t[3 * h:3 * h + 1] * o_cmp[:, hs] + gate_t[3 * h + 1:3 * h + 2] * o_sel[:, hs]
             + gate_t[3 * h + 2:3 * h + 3] * o_win[:, hs])
        ms = jnp.mean(o * o, axis=0, keepdims=True)
        outs.append(o * lax.rsqrt(ms + RMS_EPS) * og_ref[h * dh:(h + 1) * dh, :])
    o_ref[...] = jnp.concatenate(outs, axis=0).T


def _nsa_attn(q_a, k_cmp, v_cmp, ks, vs_t, kw, vw_t, small, q_g, out_g_b, batch, seq_len):
    n = q_a.shape[0]
    g, hg, dh, tq = NSA_KV_GROUPS, NSA_HG, NSA_DH, ATT_TQ
    nq = seq_len // tq
    n_cmp = seq_len // CMP_STRIDE
    n_blk = seq_len // SLC_BLOCK
    rows = hg * tq
    k_full = pl.BlockSpec((None, None, seq_len, dh), lambda b, j, i: (b, j, 0, 0))
    vt_full = pl.BlockSpec((None, None, seq_len // LANES, dh, LANES), lambda b, j, i: (b, j, 0, 0, 0))
    cmp_full = pl.BlockSpec((None, None, n_cmp, dh), lambda b, j, i: (b, j, 0, 0))
    return pl.pallas_call(
        functools.partial(_nsa_kernel, seq_len=seq_len),
        grid=(batch, g, nq),
        in_specs=[pl.BlockSpec((tq, hg * dh), lambda b, j, i: (b * nq + i, j)),
                  cmp_full, cmp_full, k_full, vt_full, k_full, vt_full,
                  pl.BlockSpec((tq, LANES), lambda b, j, i: (b * nq + i, j)),
                  pl.BlockSpec((1, dh), lambda b, j, i: (0, 0)),
                  pl.BlockSpec((None, hg * dh, tq), lambda b, j, i: (j, 0, 0))],
        out_specs=pl.BlockSpec((tq, hg * dh), lambda b, j, i: (b * nq + i, j)),
        out_shape=jax.ShapeDtypeStruct((n, NSA_WIDTH), F32),
        scratch_shapes=[pltpu.VMEM((n_blk, tq), F32), pltpu.VMEM((n_blk, tq), F32),
                        pltpu.VMEM((seq_len, rows), F32), pltpu.VMEM((dh, rows), F32)],
        compiler_params=_params("parallel", "parallel", "arbitrary"),
        name="nsa_attn",
    )(q_a, k_cmp, v_cmp, ks, vs_t, kw, vw_t, small, q_g, out_g_b)


def _mlstm_kernel(qk_ref, v_ref, o_ref, small_ref, ift_ref, cw_ref, cb_ref, fb_ref, fbt_ref, og_ref,
                  y_ref, xbuf, c_st, n_st, m_st):
    lc = M_CHUNK
    nh, dh, width = M_HEADS, M_DH, M_WIDTH
    c = pl.program_id(1)

    @pl.when(c == 0)
    def _():
        xbuf[0:SUBLANES, :] = jnp.zeros((SUBLANES, 2 * width), F32)
        c_st[...] = jnp.zeros(c_st.shape, F32)
        n_st[...] = jnp.zeros(n_st.shape, F32)
        m_st[...] = jnp.zeros(m_st.shape, F32)

    u = qk_ref[...]
    xbuf[SUBLANES:SUBLANES + lc, :] = u
    y = cb_ref[...] + cw_ref[0:1, :] * xbuf[pl.ds(SUBLANES - (CONV_WIDTH - 1), lc), :]
    for j in range(1, CONV_WIDTH):
        y = y + cw_ref[j:j + 1, :] * xbuf[pl.ds(SUBLANES - (CONV_WIDTH - 1) + j, lc), :]
    xbuf[0:SUBLANES, :] = u[lc - SUBLANES:lc, :]
    qk = _silu(y)

    r_i = lax.broadcasted_iota(jnp.int32, (lc, lc), 0)
    c_i = lax.broadcasted_iota(jnp.int32, (lc, lc), 1)
    causal = c_i <= r_i
    tril = jnp.where(causal, 1.0, 0.0).astype(BF16)
    triu = jnp.where(r_i <= c_i, 1.0, 0.0).astype(BF16)
    small = small_ref[...]
    lf_hi, lf_lo = _split_hi_lo(_log_sigmoid(small + fb_ref[...]))
    b_col_all = _dot(tril, lf_hi) + _dot(tril, lf_lo)
    ift = ift_ref[...]
    lft_hi, lft_lo = _split_hi_lo(_log_sigmoid(ift + fbt_ref[:, 0:1]))
    b_row_all = _dot(lft_hi, triu) + _dot(lft_lo, triu)

    v_all = v_ref[...]
    og_all = _sigmoid(o_ref[...])
    for h in range(nh):
        hs = slice(h * dh, (h + 1) * dh)
        q_h = qk[:, h * dh:(h + 1) * dh]
        k_h = qk[:, width + h * dh:width + (h + 1) * dh] * (dh ** -0.5)
        v_h = v_all[:, hs]
        q_b, k_b, v_b = q_h.astype(BF16), k_h.astype(BF16), v_h.astype(BF16)
        b_col = b_col_all[:, F_COL + h:F_COL + h + 1]
        i_col = small[:, I_COL + h:I_COL + h + 1]
        b_row = b_row_all[nh + h:nh + h + 1, :]
        i_row = ift[h:h + 1, :]
        m_prev = m_st[h:h + 1, 0:1]
        c_prev = c_st[h]
        n_prev = n_st[h:h + 1, :]

        a = b_col + m_prev
        d = jnp.where(causal, b_col + (i_row - b_row), NEG_INF)
        m_t = jnp.maximum(a, jnp.max(d, axis=-1, keepdims=True))
        w_inter = jnp.exp(a - m_t)
        w_intra = jnp.exp(d - m_t) * _dot_nt(q_b, k_b)
        num = w_inter * _dot(q_b, c_prev.astype(BF16)) + _dot(w_intra.astype(BF16), v_b)
        den = (w_inter * jnp.sum(q_h * n_prev, axis=-1, keepdims=True)
               + jnp.sum(w_intra, axis=-1, keepdims=True))
        hid = num * (1.0 / jnp.maximum(jnp.abs(den), jnp.exp(-m_t)))

        b_last = b_col[lc - 1:lc, :]
        g_col = b_last - b_col + i_col
        m_new = jnp.maximum(b_last + m_prev, jnp.max(g_col, axis=0, keepdims=True))
        decay = jnp.exp(b_last + m_prev - m_new)
        w_s = jnp.exp(g_col - m_new)
        c_st[h] = decay * c_prev + _dot_tn(k_b, (w_s * v_h).astype(BF16))
        n_st[h:h + 1, :] = decay * n_prev + jnp.sum(w_s * k_h, axis=0, keepdims=True)
        m_st[h:h + 1, :] = jnp.broadcast_to(m_new, (1, LANES))

        y_ref[:, hs] = og_all[:, hs] * _rms_rows(hid, og_ref[:, hs])


def _mlstm(qk_b, v_b, o_b, small, if_t, conv_w, conv_b, fb_col, fb_row, out_g, batch, seq_len):
    n = qk_b.shape[0]
    lc = M_CHUNK
    nc = seq_len // lc
    row = lambda b, c: (b * nc + c, 0)
    return pl.pallas_call(
        _mlstm_kernel,
        grid=(batch, nc),
        in_specs=[pl.BlockSpec((lc, 2 * M_WIDTH), row),
                  pl.BlockSpec((lc, M_WIDTH), row),
                  pl.BlockSpec((lc, M_WIDTH), row),
                  pl.BlockSpec((lc, LANES), lambda b, c: (b * nc + c, 1)),
                  pl.BlockSpec((None, 2 * M_HEADS, lc), lambda b, c: (b, 0, c)),
                  pl.BlockSpec((CONV_WIDTH, 2 * M_WIDTH), lambda b, c: (0, 0)),
                  pl.BlockSpec((1, 2 * M_WIDTH), lambda b, c: (0, 0)),
                  pl.BlockSpec((1, LANES), lambda b, c: (0, 0)),
                  pl.BlockSpec((2 * M_HEADS, LANES), lambda b, c: (0, 0)),
                  pl.BlockSpec((1, M_WIDTH), lambda b, c: (0, 0))],
        out_specs=pl.BlockSpec((lc, M_WIDTH), row),
        out_shape=jax.ShapeDtypeStruct((n, M_WIDTH), F32),
        scratch_shapes=[pltpu.VMEM((lc + SUBLANES, 2 * M_WIDTH), F32),
                        pltpu.VMEM((M_HEADS, M_DH, M_DH), F32),
                        pltpu.VMEM((SUBLANES, M_DH), F32),
                        pltpu.VMEM((SUBLANES, LANES), F32)],
        compiler_params=_params("parallel", "arbitrary"),
        name="mlstm",
    )(qk_b, v_b, o_b, small, if_t, conv_w, conv_b, fb_col, fb_row, out_g)


def _out_ffn_kernel(x_ref, ya_ref, yb_ref, wo_ref, g_ref, wgu_ref, wd_ref, o_ref):
    x1 = (x_ref[...] + _dot(ya_ref[...].astype(BF16), wo_ref[0:NSA_WIDTH, :])
          + _dot(yb_ref[...].astype(BF16), wo_ref[NSA_WIDTH:, :]))
    h = _rms_rows(x1, g_ref[...]).astype(BF16)
    o_ref[...] = x1
    for c in range(FFN_HIDDEN // FFN_FC):
        cs = slice(c * FFN_FC, (c + 1) * FFN_FC)
        us = slice(FFN_HIDDEN + c * FFN_FC, FFN_HIDDEN + (c + 1) * FFN_FC)
        act = (_silu(_dot(h, wgu_ref[:, cs])) * _dot(h, wgu_ref[:, us])).astype(BF16)
        o_ref[...] += _dot(act, wd_ref[cs, :])


def _out_ffn(x2, y_a, y_b, w_out, ln_g, w_gu, w_down):
    n = x2.shape[0]
    tm = min(PROJ_TM, n)
    row = lambda i: (i, 0)
    return pl.pallas_call(
        _out_ffn_kernel,
        grid=(n // tm,),
        in_specs=[pl.BlockSpec((tm, D_MODEL), row),
                  pl.BlockSpec((tm, NSA_WIDTH), row),
                  pl.BlockSpec((tm, M_WIDTH), row),
                  _resident((D_MODEL, D_MODEL)),
                  _resident((1, D_MODEL)),
                  _resident((D_MODEL, 2 * FFN_HIDDEN)),
                  _resident((FFN_HIDDEN, D_MODEL))],
        out_specs=pl.BlockSpec((tm, D_MODEL), row),
        out_shape=jax.ShapeDtypeStruct((n, D_MODEL), F32),
        compiler_params=_params("parallel"),
        name="out_ffn",
    )(x2, y_a, y_b, w_out, ln_g[None, :], w_gu, w_down)


def _layer(x2, batch, seq_len, ln1_g, w_in, b_in, q_g, k_g, pe, w1, w2, conv_w, conv_b, fgate_b,
           nsa_out_g, m_out_g, w_out, ln2_g, w_gu, w_down):
    g, dh = NSA_KV_GROUPS, NSA_DH
    w_perm, b_perm = _permute_in_proj(w_in, b_in)
    q_a, kv, qk_b, v_b, o_b, small = _in_proj(x2, ln1_g, w_perm, b_perm)

    kv6 = kv.reshape(batch, seq_len, 6, g, dh).transpose(2, 0, 3, 1, 4)
    n_cmp = seq_len // CMP_STRIDE
    cmp_in = kv6[0:2].reshape(2, batch, g, n_cmp, CMP_STRIDE * dh)
    cmp_out = _compress(cmp_in, pe.reshape(2, 2, CMP_STRIDE * dh), w1.astype(BF16), w2.astype(BF16),
                        k_g[0:1])
    k_normed = _knorm(jnp.stack([kv6[2], kv6[4]]).reshape(2, batch * g * seq_len, dh),
                      k_g[1:3, None, :]).reshape(2, batch, g, seq_len, dh)

    def v_tiled(v):
        return v.astype(BF16).reshape(batch, g, seq_len // LANES, LANES, dh).swapaxes(-1, -2)

    out_g_b = jnp.broadcast_to(nsa_out_g.reshape(g, NSA_HG * dh, 1), (g, NSA_HG * dh, ATT_TQ))
    y_a = _nsa_attn(q_a, cmp_out[0], cmp_out[1], k_normed[0], v_tiled(kv6[3]), k_normed[1],
                    v_tiled(kv6[5]), small, q_g[None, :], out_g_b, batch, seq_len)

    if_t = small[:, LANES + I_COL:LANES + I_COL + 2 * M_HEADS].reshape(
        batch, seq_len, 2 * M_HEADS).transpose(0, 2, 1)
    fb_col = jnp.zeros((1, LANES), F32).at[0, F_COL:F_COL + M_HEADS].set(fgate_b)
    fb_row = jnp.zeros((2 * M_HEADS, LANES), F32).at[M_HEADS:, :].set(
        jnp.broadcast_to(fgate_b[:, None], (M_HEADS, LANES)))
    y_b = _mlstm(qk_b, v_b, o_b, small, if_t, conv_w, conv_b[None, :], fb_col, fb_row,
                 m_out_g[None, :], batch, seq_len)

    return _out_ffn(x2, y_a, y_b, w_out.astype(BF16), ln2_g, w_gu.astype(BF16), w_down.astype(BF16))


def kernel(x, ln1_g, w_in, b_in, nsa_q_norm_g, nsa_k_norm_g, cmp_pe, cmp_w1, cmp_w2, m_conv_w,
           m_conv_b, m_fgate_b, nsa_out_norm_g, m_out_norm_g, w_out, ln2_g, w_gate_up, w_down):
    batch, seq_len, d = x.shape
    x2 = x.reshape(batch * seq_len, d)
    for l in range(ln1_g.shape[0]):
        x2 = _layer(x2, batch, seq_len, ln1_g[l], w_in[l], b_in[l], nsa_q_norm_g[l], nsa_k_norm_g[l],
                    cmp_pe[l], cmp_w1[l], cmp_w2[l], m_conv_w[l], m_conv_b[l], m_fgate_b[l],
                    nsa_out_norm_g[l], m_out_norm_g[l], w_out[l], ln2_g[l], w_gate_up[l], w_down[l])
    return x2.reshape(batch, seq_len, d)
```

```python
import functools

import jax
import jax.numpy as jnp
from jax import lax
from jax.experimental import pallas as pl
from jax.experimental.pallas import tpu as pltpu

F32 = jnp.float32
BF16 = jnp.bfloat16

D_MODEL = 1024
NSA_HEADS = 8
NSA_KV_GROUPS = 2
NSA_HG = NSA_HEADS // NSA_KV_GROUPS
NSA_DH = 64
NSA_WIDTH = NSA_HEADS * NSA_DH
NSA_KV_WIDTH = NSA_KV_GROUPS * NSA_DH
CMP_BLOCK = 32
CMP_STRIDE = 16
CMP_HIDDEN = 2 * NSA_DH
SLC_BLOCK = 64
SLC_TOPN = 16
WINDOW = 512
FORCE_BONUS = 1e4
M_HEADS = 4
M_WIDTH = 512
M_DH = 128
CONV_WIDTH = 4
FFN_HIDDEN = 2816
RMS_EPS = 1e-6
NEG_INF = -1e30
LOG2E = 1.4426950408889634

LANES = 128
SUBLANES = 8
VMEM_LIMIT = 56 * 1024 * 1024

PROJ_TM = 512
ATT_TQ = 128
SEL_BODY = 512
DEAD_COL = 64
M_CHUNK = 256
FFN_FC = 256

SMALL_W = 2 * LANES
GATE_COLS = 3 * NSA_HG
I_COL = GATE_COLS
F_COL = GATE_COLS + M_HEADS


def _dot(a, b):
    return jnp.dot(a, b, preferred_element_type=F32)


def _dot_nt(a, b):
    return lax.dot_general(a, b, (((1,), (1,)), ((), ())), preferred_element_type=F32)


def _dot_tn(a, b):
    return lax.dot_general(a, b, (((0,), (0,)), ((), ())), preferred_element_type=F32)


def _split_hi_lo(a):
    hi = a.astype(BF16)
    lo = (a - hi.astype(F32)).astype(BF16)
    return hi, lo


def _silu(x):
    return x * (1.0 / (1.0 + jnp.exp(-x)))


def _sigmoid(x):
    return 1.0 / (1.0 + jnp.exp(-x))


def _log_sigmoid(x):
    return jnp.minimum(x, 0.0) - jnp.log(1.0 + jnp.exp(-jnp.abs(x)))


def _rms_rows(x, g):
    ms = jnp.mean(x * x, axis=-1, keepdims=True)
    return x * lax.rsqrt(ms + RMS_EPS) * g


def _params(*sem):
    return pltpu.CompilerParams(dimension_semantics=sem, vmem_limit_bytes=VMEM_LIMIT)


def _resident(shape):
    nd = len(shape)
    return pl.BlockSpec(shape, lambda *_: (0,) * nd, pipeline_mode=pl.Buffered(1))


IN_SEGS = (("q_a", NSA_WIDTH), ("kv", 6 * NSA_KV_WIDTH), ("qk_b", 2 * M_WIDTH),
           ("v_b", M_WIDTH), ("o_b", M_WIDTH), ("small", SMALL_W))
IN_COLS = sum(w for _, w in IN_SEGS)


def _in_proj_kernel(x_ref, g_ref, w_ref, b_ref, *out_refs):
    x = x_ref[...]
    h = _rms_rows(x, g_ref[...]).astype(BF16)
    c0 = 0
    for o_ref, (_, width) in zip(out_refs, IN_SEGS):
        o_ref[...] = _dot(h, w_ref[:, c0:c0 + width]) + b_ref[:, c0:c0 + width]
        c0 += width


def _permute_in_proj(w_in, b_in):
    o = 0
    parts = {}
    for name, width in (("q_a", NSA_WIDTH), ("kv", 6 * NSA_KV_WIDTH), ("g_a", 3 * NSA_HEADS),
                        ("q_b", M_WIDTH), ("k_b", M_WIDTH), ("v_b", M_WIDTH),
                        ("i_b", M_HEADS), ("f_b", M_HEADS), ("o_b", M_WIDTH)):
        parts[name] = (o, o + width)
        o += width

    def perm(a):
        def sl(name):
            lo, hi = parts[name]
            return a[..., lo:hi]
        lead = a.shape[:-1]
        g_a = sl("g_a")
        small0 = jnp.concatenate([g_a[..., :GATE_COLS],
                                  jnp.zeros(lead + (LANES - GATE_COLS,), a.dtype)], axis=-1)
        small1 = jnp.concatenate([g_a[..., GATE_COLS:], sl("i_b"), sl("f_b"),
                                  jnp.zeros(lead + (LANES - GATE_COLS - 2 * M_HEADS,), a.dtype)], axis=-1)
        return jnp.concatenate([sl("q_a"), sl("kv"), sl("q_b"), sl("k_b"), sl("v_b"), sl("o_b"),
                                small0, small1], axis=-1)

    return perm(w_in).astype(BF16), perm(b_in)[None, :]


def _in_proj(x2, ln_g, w_perm, b_perm):
    n = x2.shape[0]
    tm = min(PROJ_TM, n)
    out_shape = [jax.ShapeDtypeStruct((n, w), F32) for _, w in IN_SEGS]
    out_specs = [pl.BlockSpec((tm, w), lambda i: (i, 0)) for _, w in IN_SEGS]
    return pl.pallas_call(
        _in_proj_kernel,
        grid=(n // tm,),
        in_specs=[pl.BlockSpec((tm, D_MODEL), lambda i: (i, 0)),
                  _resident((1, D_MODEL)),
                  _resident((D_MODEL, IN_COLS)),
                  _resident((1, IN_COLS))],
        out_specs=out_specs,
        out_shape=out_shape,
        compiler_params=_params("parallel"),
        name="in_proj",
    )(x2, ln_g[None, :], w_perm, b_perm)


def _cmp_kernel(k_ref, v_ref, pe_ref, w1_ref, w2k_ref, w2vt_ref, kg_ref, kc_ref, vct_ref, *, n_cmp):
    acc = [[jnp.zeros((n_cmp, 2 * CMP_HIDDEN), F32) for _ in range(2)] for _ in range(2)]
    for l in range(CMP_STRIDE):
        for w, x_ref in enumerate((k_ref, v_ref)):
            xw = x_ref[pl.ds(l, n_cmp, stride=CMP_STRIDE), :]
            for half in range(2):
                pos = half * CMP_STRIDE + l
                acc[w][half] = acc[w][half] + _dot((xw + pe_ref[w, pos:pos + 1, :]).astype(BF16),
                                                   w1_ref[w, pos])
    act = [_silu(acc[w][0] + pltpu.roll(acc[w][1], n_cmp - 1, 0)).astype(BF16) for w in range(2)]
    k_out = _dot(act[0], w2k_ref[...])
    k_out = jnp.concatenate([_rms_rows(k_out[:, g * NSA_DH:(g + 1) * NSA_DH], kg_ref[...])
                             for g in range(NSA_KV_GROUPS)], axis=1)
    kc_ref[...] = k_out.astype(BF16)
    vct_ref[...] = _dot_nt(w2vt_ref[...], act[1]).astype(BF16)


def _block_diag2(w):
    z = jnp.zeros_like(w)
    return jnp.concatenate([jnp.concatenate([w, z], axis=-1), jnp.concatenate([z, w], axis=-1)], axis=-2)


def _compress(kv, pe, w1, w2, kg, batch, seq_len):
    n_cmp = seq_len // CMP_STRIDE
    dh, gw = NSA_DH, NSA_KV_WIDTH
    pe2 = jnp.concatenate([pe, pe], axis=-1)
    w1b = _block_diag2(w1.reshape(2, CMP_BLOCK, dh, CMP_HIDDEN)).astype(BF16)
    w2b = _block_diag2(w2).astype(BF16)
    w2k, w2v_t = w2b[0], w2b[1].T
    return pl.pallas_call(
        functools.partial(_cmp_kernel, n_cmp=n_cmp),
        grid=(batch,),
        in_specs=[pl.BlockSpec((seq_len, gw), lambda b: (b, 0)),
                  pl.BlockSpec((seq_len, gw), lambda b: (b, 1)),
                  pl.BlockSpec((2, CMP_BLOCK, gw), lambda b: (0, 0, 0)),
                  pl.BlockSpec((2, CMP_BLOCK, gw, 2 * CMP_HIDDEN), lambda b: (0, 0, 0, 0)),
                  pl.BlockSpec((2 * CMP_HIDDEN, gw), lambda b: (0, 0)),
                  pl.BlockSpec((gw, 2 * CMP_HIDDEN), lambda b: (0, 0)),
                  pl.BlockSpec((1, dh), lambda b: (0, 0))],
        out_specs=[pl.BlockSpec((None, n_cmp, gw), lambda b: (b, 0, 0)),
                   pl.BlockSpec((None, gw, n_cmp), lambda b: (b, 0, 0))],
        out_shape=[jax.ShapeDtypeStruct((batch, n_cmp, gw), BF16),
                   jax.ShapeDtypeStruct((batch, gw, n_cmp), BF16)],
        compiler_params=_params("parallel"),
        name="cmp_kv",
    )(kv, kv, pe2, w1b, w2k, w2v_t, kg)


def _attn_prep_kernel(q_ref, sel_ref, win_ref, qg_ref, kg_ref, qt_ref, ksa_ref, kw_ref, vst_ref, vwt_ref):
    tm = q_ref.shape[0]
    dh, gw, hg = NSA_DH, NSA_KV_WIDTH, NSA_HG
    t_base = pl.program_id(1) * tm
    sel, win = sel_ref[...], win_ref[...]
    seg_r = lax.broadcasted_iota(jnp.int32, (gw, gw), 0) // dh
    seg_c = lax.broadcasted_iota(jnp.int32, (gw, gw), 1) // dh
    seg_ones = jnp.where(seg_r == seg_c, 1.0, 0.0).astype(BF16)

    def head_norm(x, g):
        hi, lo = _split_hi_lo(x * x)
        ss = _dot(hi, seg_ones) + _dot(lo, seg_ones)
        return (x * lax.rsqrt(ss * (1.0 / dh) + RMS_EPS) * g).astype(BF16)

    is_dummy = pl.program_id(1) == pl.num_programs(1) - 1
    key = t_base + lax.broadcasted_iota(jnp.int32, (tm, gw), 0)
    col = lax.broadcasted_iota(jnp.int32, (tm, gw), 1)
    onehot = jnp.where(col == jnp.where(is_dummy, DEAD_COL, key // SLC_BLOCK), 1.0, 0.0).astype(BF16)
    k_sel = head_norm(sel[:, 0:gw], kg_ref[0:1, :])
    ksa_ref[...] = jnp.concatenate([jnp.where(is_dummy, jnp.zeros_like(k_sel), k_sel), onehot], axis=1)
    kw_ref[...] = head_norm(win[:, 0:gw], kg_ref[1:2, :])

    zeros = jnp.zeros((dh, hg * LANES), BF16)
    for j in range(tm // LANES):
        r = slice(j * LANES, (j + 1) * LANES)
        vst_ref[j] = sel[r, gw:].T.astype(BF16)
        vwt_ref[j] = win[r, gw:].T.astype(BF16)
        x_t = q_ref[r, :].T
        for g in range(NSA_KV_GROUPS):
            heads = []
            for h in range(hg):
                seg = x_t[(g * hg + h) * dh:(g * hg + h + 1) * dh, :]
                ms = jnp.mean(seg * seg, axis=0, keepdims=True)
                heads.append(seg * lax.rsqrt(ms + RMS_EPS) * qg_ref[...] * (dh ** -0.5 * LOG2E))
            blockq = jnp.concatenate(heads, axis=1).astype(BF16)
            qt_ref[j, g] = jnp.concatenate([blockq, zeros] if g == 0 else [zeros, blockq], axis=0)


def _attn_prep(q_a, kv, q_g, k_g, batch, seq_len):
    tm = SEL_BODY
    nt = seq_len // tm
    gw, hg = NSA_KV_WIDTH, NSA_HG
    n_lt = seq_len // LANES
    qg_b = jnp.broadcast_to(q_g[:, None], (NSA_DH, LANES))
    kg_b = jnp.concatenate([k_g[1:3], k_g[1:3]], axis=1)
    last = lambda c: jnp.minimum(c, nt - 1)
    return pl.pallas_call(
        _attn_prep_kernel,
        grid=(batch, nt + 1),
        in_specs=[pl.BlockSpec((tm, NSA_WIDTH), lambda b, c: (b * nt + last(c), 0)),
                  pl.BlockSpec((tm, 2 * gw), lambda b, c: (b * nt + last(c), 1)),
                  pl.BlockSpec((tm, 2 * gw), lambda b, c: (b * nt + last(c), 2)),
                  pl.BlockSpec((NSA_DH, LANES), lambda b, c: (0, 0)),
                  pl.BlockSpec((2, gw), lambda b, c: (0, 0))],
        out_specs=[pl.BlockSpec((None, tm // LANES, NSA_KV_GROUPS, gw, hg * LANES),
                                lambda b, c: (b, last(c), 0, 0, 0)),
                   pl.BlockSpec((None, tm, 2 * gw), lambda b, c: (b, c, 0)),
                   pl.BlockSpec((None, tm, gw), lambda b, c: (b, last(c), 0)),
                   pl.BlockSpec((None, tm // LANES, gw, LANES), lambda b, c: (b, last(c), 0, 0)),
                   pl.BlockSpec((None, tm // LANES, gw, LANES), lambda b, c: (b, last(c), 0, 0))],
        out_shape=[jax.ShapeDtypeStruct((batch, n_lt, NSA_KV_GROUPS, gw, hg * LANES), BF16),
                   jax.ShapeDtypeStruct((batch, seq_len + tm, 2 * gw), BF16),
                   jax.ShapeDtypeStruct((batch, seq_len, gw), BF16),
                   jax.ShapeDtypeStruct((batch, n_lt, gw, LANES), BF16),
                   jax.ShapeDtypeStruct((batch, n_lt, gw, LANES), BF16)],
        compiler_params=_params("parallel", "parallel"),
        name="attn_prep",
    )(q_a, kv, kv, qg_b, kg_b)


def _softmax_pv(s, v_t):
    m = jnp.max(s, axis=0, keepdims=True)
    p = jnp.exp2(s - m)
    l = jnp.sum(p, axis=0, keepdims=True)
    return _dot(v_t, p.astype(BF16)) * (1.0 / l)


def _nsa_kernel(qt_ref, kc_ref, vct_ref, ksa_ref, vst_ref, kw_ref, vwt_ref, gate_ref, og_ref,
                o_ref, score_ref, qaug_ref, s0, s1, s_last, p0, p1, a0, a1, m_ref, l_ref, acc_ref,
                *, seq_len):
    tq, hg, dh, gw = ATT_TQ, NSA_HG, NSA_DH, NSA_KV_WIDTH
    rows = hg * tq
    n_cmp = seq_len // CMP_STRIDE
    n_blk = seq_len // SLC_BLOCK
    n_sel = min(SLC_TOPN, n_blk)
    body = SEL_BODY
    i = pl.program_id(1)
    t0 = pl.multiple_of(i * tq, tq)
    qt = qt_ref[...]

    def lane_tile(a):
        return jnp.concatenate([a] * hg, axis=1)

    def v_tiles(vt_ref, k0, tk):
        return jnp.concatenate([vt_ref[k0 // LANES + jt] for jt in range(tk // LANES)], axis=1)

    def fold8(a):
        return a.reshape(a.shape[0] // SUBLANES, SUBLANES, rows)

    span = WINDOW + tq
    w0 = pl.multiple_of(jnp.clip(t0 - WINDOW, 0, seq_len - span), tq)
    k_pos = w0 + lax.broadcasted_iota(jnp.int32, (span, tq), 0)
    q_pos = t0 + lax.broadcasted_iota(jnp.int32, (span, tq), 1)
    w_bias = jnp.where(k_pos <= q_pos, jnp.where(k_pos > q_pos - WINDOW, 0.0, NEG_INF), NEG_INF)
    s_w = _dot(kw_ref[pl.ds(w0, span), :], qt) + lane_tile(w_bias)
    o_win = _softmax_pv(s_w, v_tiles(vwt_ref, w0, span))

    c_end = lax.broadcasted_iota(jnp.int32, (n_cmp, tq), 0) * CMP_STRIDE + (CMP_BLOCK - 1)
    c_ok = c_end <= t0 + lax.broadcasted_iota(jnp.int32, (n_cmp, tq), 1)
    s_c = _dot(kc_ref[...], qt) + lane_tile(jnp.where(c_ok, 0.0, NEG_INF))
    m_c = jnp.max(s_c, axis=0, keepdims=True)
    e_c = jnp.exp2(s_c - m_c) * lane_tile(jnp.where(c_ok, 1.0, 0.0))
    l_c = jnp.sum(e_c, axis=0, keepdims=True)
    p_c = e_c * (1.0 / jnp.where(l_c > 0.0, l_c, 1.0))
    o_cmp = _dot(vct_ref[...], p_c.astype(BF16))

    p_sum = p_c[:, 0:tq]
    for h in range(1, hg):
        p_sum = p_sum + p_c[:, h * tq:(h + 1) * tq]
    blk_r = lax.broadcasted_iota(jnp.int32, (n_blk, n_cmp), 0)
    cmp_c = lax.broadcasted_iota(jnp.int32, (n_blk, n_cmp), 1)
    overlap = ((cmp_c * CMP_STRIDE < (blk_r + 1) * SLC_BLOCK)
               & (cmp_c * CMP_STRIDE + (CMP_BLOCK - 1) >= blk_r * SLC_BLOCK)
               & (cmp_c < n_cmp - 1))
    overlap = jnp.where(overlap, 1.0, 0.0).astype(BF16)
    p_hi, p_lo = _split_hi_lo(p_sum)
    imp = _dot(overlap, p_hi) + _dot(overlap, p_lo)

    blk = lax.broadcasted_iota(jnp.int32, (n_blk, tq), 0)
    t_col = t0 + lax.broadcasted_iota(jnp.int32, (n_blk, tq), 1)
    cur = t_col // SLC_BLOCK
    forced = (blk == 0) | (blk == cur) | (blk == cur - 1)
    score = jnp.where(blk * SLC_BLOCK <= t_col,
                      imp + jnp.where(forced, FORCE_BONUS, 0.0), NEG_INF)
    score_ref[...] = score

    def rank_body(jp, rank):
        row = score_ref[pl.ds(jp, 1), :]
        return rank + jnp.where(blk > jp, jnp.where(row >= score, 1.0, 0.0),
                                jnp.where(row > score, 1.0, 0.0))

    n_live = jnp.minimum((t0 + tq) // SLC_BLOCK, n_blk)
    rank = lax.fori_loop(0, n_live, rank_body, jnp.zeros((n_blk, tq), F32))
    neg = jnp.where(rank < float(n_sel), 0.0, NEG_INF)

    dead = lax.broadcasted_iota(jnp.int32, (gw - n_blk, rows), 0) == DEAD_COL - n_blk
    qaug_ref[0:gw, :] = qt
    qaug_ref[gw:gw + n_blk, :] = lane_tile(neg).astype(BF16)
    qaug_ref[gw + n_blk:2 * gw, :] = jnp.where(dead, NEG_INF, 0.0).astype(BF16)
    q_aug = qaug_ref[...]

    n_full = t0 // body
    k_last = pl.multiple_of(n_full * body, body)

    def scores(n):
        k0 = pl.multiple_of(jnp.where(n < n_full, n * body, seq_len), body)
        return _dot(ksa_ref[pl.ds(k0, body), :], q_aug)

    def pv(n, p_ref, a_ref):
        k0 = pl.multiple_of(jnp.clip(n, 0, seq_len // body - 1) * body, body)
        acc_ref[...] = a_ref[...] * acc_ref[...] + _dot(v_tiles(vst_ref, k0, body), p_ref[...])

    def softmax(s_ref, mx8, p_ref, a_ref):
        m_old = m_ref[...]
        m_new = jnp.maximum(m_old, jnp.max(mx8, axis=0, keepdims=True))
        alpha = jnp.exp2(m_old - m_new)
        p = jnp.exp2(s_ref[...] - m_new)
        l_ref[...] = alpha * l_ref[...] + jnp.sum(fold8(p), axis=0)
        m_ref[...] = m_new
        a_ref[...] = alpha
        p_ref[...] = p.astype(BF16)

    def stage(n, s_cur, mx_cur, p_prev, a_prev, p_cur, a_cur, s_next):
        pv(n - 1, p_prev, a_prev)
        softmax(s_cur, mx_cur, p_cur, a_cur)
        s = scores(n + 1)
        s_next[...] = s
        return jnp.max(fold8(s), axis=0)

    m_ref[...] = jnp.full((1, rows), NEG_INF, F32)
    l_ref[...] = jnp.zeros((SUBLANES, rows), F32)
    acc_ref[...] = jnp.zeros((dh, rows), F32)
    p1[...] = jnp.zeros((body, rows), BF16)
    a1[...] = jnp.ones((1, rows), F32)
    k_b = k_last + lax.broadcasted_iota(jnp.int32, (body, tq), 0)
    q_b = t0 + lax.broadcasted_iota(jnp.int32, (body, tq), 1)
    s = _dot(ksa_ref[pl.ds(k_last, body), :], q_aug) + lane_tile(jnp.where(k_b <= q_b, 0.0, NEG_INF))
    s_last[...] = s
    mx_last = jnp.max(fold8(s), axis=0)
    s = scores(0)
    s0[...] = s

    def sel_body(j, mx0):
        mx1 = stage(2 * j, s0, mx0, p1, a1, p0, a0, s1)
        return stage(2 * j + 1, s1, mx1, p0, a0, p1, a1, s0)

    n_trips = (n_full + 1) // 2
    lax.fori_loop(0, n_trips, sel_body, jnp.max(fold8(s), axis=0))
    pv(2 * n_trips - 1, p1, a1)
    softmax(s_last, mx_last, p0, a0)
    pv(n_full, p0, a0)
    o_sel = acc_ref[...] * (1.0 / jnp.sum(l_ref[...], axis=0, keepdims=True))

    gate_t = _sigmoid(gate_ref[...]).T
    outs = []
    for h in range(hg):
        hs = slice(h * tq, (h + 1) * tq)
        o = (gate_t[3 * h:3 * h + 1] * o_cmp[:, hs] + gate_t[3 * h + 1:3 * h + 2] * o_sel[:, hs]
             + gate_t[3 * h + 2:3 * h + 3] * o_win[:, hs])
        ms = jnp.mean(o * o, axis=0, keepdims=True)
        outs.append(o * lax.rsqrt(ms + RMS_EPS) * og_ref[h * dh:(h + 1) * dh, :])
    o_ref[...] = jnp.concatenate(outs, axis=0).T


def _nsa_attn(q_t, k_cmp, v_cmp_t, ks_aug, vs_t, kw, vw_t, small, out_g_b, batch, seq_len):
    g, hg, dh, tq, gw = NSA_KV_GROUPS, NSA_HG, NSA_DH, ATT_TQ, NSA_KV_WIDTH
    nq = seq_len // tq
    n_cmp = seq_len // CMP_STRIDE
    n_blk = seq_len // SLC_BLOCK
    n_lt = seq_len // LANES
    rows = hg * tq
    assert n_blk <= DEAD_COL and tq == LANES and seq_len % SEL_BODY == 0
    vt_spec = pl.BlockSpec((None, n_lt, dh, LANES), lambda b, i, j: (b, 0, j, 0))
    return pl.pallas_call(
        functools.partial(_nsa_kernel, seq_len=seq_len),
        grid=(batch, nq, g),
        in_specs=[pl.BlockSpec((None, None, None, gw, rows), lambda b, i, j: (b, i, j, 0, 0)),
                  pl.BlockSpec((None, n_cmp, gw), lambda b, i, j: (b, 0, 0)),
                  pl.BlockSpec((None, dh, n_cmp), lambda b, i, j: (b, j, 0)),
                  pl.BlockSpec((None, seq_len + SEL_BODY, 2 * gw), lambda b, i, j: (b, 0, 0)),
                  vt_spec,
                  pl.BlockSpec((None, seq_len, gw), lambda b, i, j: (b, 0, 0)),
                  vt_spec,
                  pl.BlockSpec((tq, LANES), lambda b, i, j: (b * nq + i, j)),
                  pl.BlockSpec((None, hg * dh, tq), lambda b, i, j: (j, 0, 0))],
        out_specs=pl.BlockSpec((tq, hg * dh), lambda b, i, j: (b * nq + i, j)),
        out_shape=jax.ShapeDtypeStruct((batch * seq_len, NSA_WIDTH), F32),
        scratch_shapes=[pltpu.VMEM((n_blk, tq), F32),
                        pltpu.VMEM((2 * gw, rows), BF16),
                        pltpu.VMEM((SEL_BODY, rows), F32),
                        pltpu.VMEM((SEL_BODY, rows), F32),
                        pltpu.VMEM((SEL_BODY, rows), F32),
                        pltpu.VMEM((SEL_BODY, rows), BF16),
                        pltpu.VMEM((SEL_BODY, rows), BF16),
                        pltpu.VMEM((1, rows), F32),
                        pltpu.VMEM((1, rows), F32),
                        pltpu.VMEM((1, rows), F32),
                        pltpu.VMEM((SUBLANES, rows), F32),
                        pltpu.VMEM((dh, rows), F32)],
        compiler_params=_params("parallel", "arbitrary", "arbitrary"),
        name="nsa_attn",
    )(q_t, k_cmp, v_cmp_t, ks_aug, vs_t, kw, vw_t, small, out_g_b)


def _mlstm_kernel(qk_ref, v_ref, o_ref, small_ref, ift_ref, cw_ref, cb_ref, fb_ref, fbt_ref, og_ref,
                  y_ref, xbuf, c_st, n_st, m_st):
    lc = M_CHUNK
    nh, dh, width = M_HEADS, M_DH, M_WIDTH
    c = pl.program_id(1)

    @pl.when(c == 0)
    def _():
        xbuf[0:SUBLANES, :] = jnp.zeros((SUBLANES, 2 * width), F32)
        c_st[...] = jnp.zeros(c_st.shape, F32)
        n_st[...] = jnp.zeros(n_st.shape, F32)
        m_st[...] = jnp.zeros(m_st.shape, F32)

    u = qk_ref[...]
    xbuf[SUBLANES:SUBLANES + lc, :] = u
    y = cb_ref[...] + cw_ref[0:1, :] * xbuf[pl.ds(SUBLANES - (CONV_WIDTH - 1), lc), :]
    for j in range(1, CONV_WIDTH):
        y = y + cw_ref[j:j + 1, :] * xbuf[pl.ds(SUBLANES - (CONV_WIDTH - 1) + j, lc), :]
    xbuf[0:SUBLANES, :] = u[lc - SUBLANES:lc, :]
    qk = _silu(y)

    r_i = lax.broadcasted_iota(jnp.int32, (lc, lc), 0)
    c_i = lax.broadcasted_iota(jnp.int32, (lc, lc), 1)
    causal = c_i <= r_i
    tril = jnp.where(causal, 1.0, 0.0).astype(BF16)
    triu = jnp.where(r_i <= c_i, 1.0, 0.0).astype(BF16)
    small = small_ref[...]
    lf_hi, lf_lo = _split_hi_lo(_log_sigmoid(small + fb_ref[...]))
    b_col_all = _dot(tril, lf_hi) + _dot(tril, lf_lo)
    ift = ift_ref[...]
    lft_hi, lft_lo = _split_hi_lo(_log_sigmoid(ift + fbt_ref[:, 0:1]))
    b_row_all = _dot(lft_hi, triu) + _dot(lft_lo, triu)

    v_all = v_ref[...]
    og_all = _sigmoid(o_ref[...])
    for h in range(nh):
        hs = slice(h * dh, (h + 1) * dh)
        q_h = qk[:, h * dh:(h + 1) * dh]
        k_h = qk[:, width + h * dh:width + (h + 1) * dh] * (dh ** -0.5)
        v_h = v_all[:, hs]
        q_b, k_b, v_b = q_h.astype(BF16), k_h.astype(BF16), v_h.astype(BF16)
        b_col = b_col_all[:, F_COL + h:F_COL + h + 1]
        i_col = small[:, I_COL + h:I_COL + h + 1]
        b_row = b_row_all[nh + h:nh + h + 1, :]
        i_row = ift[h:h + 1, :]
        m_prev = m_st[h:h + 1, 0:1]
        c_prev = c_st[h]
        n_prev = n_st[h:h + 1, :]

        a = b_col + m_prev
        d = jnp.where(causal, b_col + (i_row - b_row), NEG_INF)
        m_t = jnp.maximum(a, jnp.max(d, axis=-1, keepdims=True))
        w_inter = jnp.exp(a - m_t)
        w_intra = jnp.exp(d - m_t) * _dot_nt(q_b, k_b)
        num = w_inter * _dot(q_b, c_prev.astype(BF16)) + _dot(w_intra.astype(BF16), v_b)
        den = (w_inter * jnp.sum(q_h * n_prev, axis=-1, keepdims=True)
               + jnp.sum(w_intra, axis=-1, keepdims=True))
        hid = num * (1.0 / jnp.maximum(jnp.abs(den), jnp.exp(-m_t)))

        b_last = b_col[lc - 1:lc, :]
        g_col = b_last - b_col + i_col
        m_new = jnp.maximum(b_last + m_prev, jnp.max(g_col, axis=0, keepdims=True))
        decay = jnp.exp(b_last + m_prev - m_new)
        w_s = jnp.exp(g_col - m_new)
        c_st[h] = decay * c_prev + _dot_tn(k_b, (w_s * v_h).astype(BF16))
        n_st[h:h + 1, :] = decay * n_prev + jnp.sum(w_s * k_h, axis=0, keepdims=True)
        m_st[h:h + 1, :] = jnp.broadcast_to(m_new, (1, LANES))

        y_ref[:, hs] = og_all[:, hs] * _rms_rows(hid, og_ref[:, hs])


def _mlstm(qk_b, v_b, o_b, small, if_t, conv_w, conv_b, fb_col, fb_row, out_g, batch, seq_len):
    n = qk_b.shape[0]
    lc = M_CHUNK
    nc = seq_len // lc
    row = lambda b, c: (b * nc + c, 0)
    return pl.pallas_call(
        _mlstm_kernel,
        grid=(batch, nc),
        in_specs=[pl.BlockSpec((lc, 2 * M_WIDTH), row),
                  pl.BlockSpec((lc, M_WIDTH), row),
                  pl.BlockSpec((lc, M_WIDTH), row),
                  pl.BlockSpec((lc, LANES), lambda b, c: (b * nc + c, 1)),
                  pl.BlockSpec((None, 2 * M_HEADS, lc), lambda b, c: (b, 0, c)),
                  pl.BlockSpec((CONV_WIDTH, 2 * M_WIDTH), lambda b, c: (0, 0)),
                  pl.BlockSpec((1, 2 * M_WIDTH), lambda b, c: (0, 0)),
                  pl.BlockSpec((1, LANES), lambda b, c: (0, 0)),
                  pl.BlockSpec((2 * M_HEADS, LANES), lambda b, c: (0, 0)),
                  pl.BlockSpec((1, M_WIDTH), lambda b, c: (0, 0))],
        out_specs=pl.BlockSpec((lc, M_WIDTH), row),
        out_shape=jax.ShapeDtypeStruct((n, M_WIDTH), F32),
        scratch_shapes=[pltpu.VMEM((lc + SUBLANES, 2 * M_WIDTH), F32),
                        pltpu.VMEM((M_HEADS, M_DH, M_DH), F32),
                        pltpu.VMEM((SUBLANES, M_DH), F32),
                        pltpu.VMEM((SUBLANES, LANES), F32)],
        compiler_params=_params("parallel", "arbitrary"),
        name="mlstm",
    )(qk_b, v_b, o_b, small, if_t, conv_w, conv_b, fb_col, fb_row, out_g)


def _out_ffn_kernel(x_ref, ya_ref, yb_ref, wo_ref, g_ref, wgu_ref, wd_ref, o_ref):
    x1 = (x_ref[...] + _dot(ya_ref[...].astype(BF16), wo_ref[0:NSA_WIDTH, :])
          + _dot(yb_ref[...].astype(BF16), wo_ref[NSA_WIDTH:, :]))
    h = _rms_rows(x1, g_ref[...]).astype(BF16)
    o_ref[...] = x1
    for c in range(FFN_HIDDEN // FFN_FC):
        cs = slice(c * FFN_FC, (c + 1) * FFN_FC)
        us = slice(FFN_HIDDEN + c * FFN_FC, FFN_HIDDEN + (c + 1) * FFN_FC)
        act = (_silu(_dot(h, wgu_ref[:, cs])) * _dot(h, wgu_ref[:, us])).astype(BF16)
        o_ref[...] += _dot(act, wd_ref[cs, :])


def _out_ffn(x2, y_a, y_b, w_out, ln_g, w_gu, w_down):
    n = x2.shape[0]
    tm = min(PROJ_TM, n)
    row = lambda i: (i, 0)
    return pl.pallas_call(
        _out_ffn_kernel,
        grid=(n // tm,),
        in_specs=[pl.BlockSpec((tm, D_MODEL), row),
                  pl.BlockSpec((tm, NSA_WIDTH), row),
                  pl.BlockSpec((tm, M_WIDTH), row),
                  _resident((D_MODEL, D_MODEL)),
                  _resident((1, D_MODEL)),
                  _resident((D_MODEL, 2 * FFN_HIDDEN)),
                  _resident((FFN_HIDDEN, D_MODEL))],
        out_specs=pl.BlockSpec((tm, D_MODEL), row),
        out_shape=jax.ShapeDtypeStruct((n, D_MODEL), F32),
        compiler_params=_params("parallel"),
        name="out_ffn",
    )(x2, y_a, y_b, w_out, ln_g[None, :], w_gu, w_down)


def _layer(x2, batch, seq_len, ln1_g, w_in, b_in, q_g, k_g, pe, w1, w2, conv_w, conv_b, fgate_b,
           nsa_out_g, m_out_g, w_out, ln2_g, w_gu, w_down):
    g, dh = NSA_KV_GROUPS, NSA_DH
    w_perm, b_perm = _permute_in_proj(w_in, b_in)
    q_a, kv, qk_b, v_b, o_b, small = _in_proj(x2, ln1_g, w_perm, b_perm)

    k_cmp, v_cmp_t = _compress(kv, pe, w1, w2, k_g[0:1], batch, seq_len)
    q_t, ks_aug, kw, vs_t, vw_t = _attn_prep(q_a, kv, q_g, k_g, batch, seq_len)
    out_g_b = jnp.broadcast_to(nsa_out_g.reshape(g, NSA_HG * dh, 1), (g, NSA_HG * dh, ATT_TQ))
    y_a = _nsa_attn(q_t, k_cmp, v_cmp_t, ks_aug, vs_t, kw, vw_t, small, out_g_b, batch, seq_len)

    if_t = small[:, LANES + I_COL:LANES + I_COL + 2 * M_HEADS].reshape(
        batch, seq_len, 2 * M_HEADS).transpose(0, 2, 1)
    fb_col = jnp.zeros((1, LANES), F32).at[0, F_COL:F_COL + M_HEADS].set(fgate_b)
    fb_row = jnp.zeros((2 * M_HEADS, LANES), F32).at[M_HEADS:, :].set(
        jnp.broadcast_to(fgate_b[:, None], (M_HEADS, LANES)))
    y_b = _mlstm(qk_b, v_b, o_b, small, if_t, conv_w, conv_b[None, :], fb_col, fb_row,
                 m_out_g[None, :], batch, seq_len)

    return _out_ffn(x2, y_a, y_b, w_out.astype(BF16), ln2_g, w_gu.astype(BF16), w_down.astype(BF16))


def kernel(x, ln1_g, w_in, b_in, nsa_q_norm_g, nsa_k_norm_g, cmp_pe, cmp_w1, cmp_w2, m_conv_w,
           m_conv_b, m_fgate_b, nsa_out_norm_g, m_out_norm_g, w_out, ln2_g, w_gate_up, w_down):
    batch, seq_len, d = x.shape
    x2 = x.reshape(batch * seq_len, d)
    for l in range(ln1_g.shape[0]):
        x2 = _layer(x2, batch, seq_len, ln1_g[l], w_in[l], b_in[l], nsa_q_norm_g[l], nsa_k_norm_g[l],
                    cmp_pe[l], cmp_w1[l], cmp_w2[l], m_conv_w[l], m_conv_b[l], m_fgate_b[l],
                    nsa_out_norm_g[l], m_out_norm_g[l], w_out[l], ln2_g[l], w_gate_up[l], w_down[l])
    return x2.reshape(batch, seq_len, d)
```

```python
import functools

import jax
import jax.numpy as jnp
from jax import lax
from jax.experimental import pallas as pl
from jax.experimental.pallas import tpu as pltpu

F32 = jnp.float32
BF16 = jnp.bfloat16

D_MODEL = 1024
NSA_HEADS = 8
NSA_KV_GROUPS = 2
NSA_HG = NSA_HEADS // NSA_KV_GROUPS
NSA_DH = 64
NSA_WIDTH = NSA_HEADS * NSA_DH
NSA_KV_WIDTH = NSA_KV_GROUPS * NSA_DH
CMP_BLOCK = 32
CMP_STRIDE = 16
CMP_HIDDEN = 2 * NSA_DH
SLC_BLOCK = 64
SLC_TOPN = 16
WINDOW = 512
FORCE_BONUS = 1e4
M_HEADS = 4
M_WIDTH = 512
M_DH = 128
CONV_WIDTH = 4
FFN_HIDDEN = 2816
RMS_EPS = 1e-6
NEG_INF = -1e30
LOG2E = 1.4426950408889634

LANES = 128
SUBLANES = 8
VMEM_LIMIT = 56 * 1024 * 1024

PROJ_TM = 512
ATT_TQ = 256
SEL_BODY = 256
DEAD_COL = 64
M_CHUNK = 256
FFN_FC = 256

SMALL_W = 2 * LANES
GATE_COLS = 3 * NSA_HG
I_COL = GATE_COLS
F_COL = GATE_COLS + M_HEADS


def _dot(a, b):
    return jnp.dot(a, b, preferred_element_type=F32)


def _dot_nt(a, b):
    return lax.dot_general(a, b, (((1,), (1,)), ((), ())), preferred_element_type=F32)


def _dot_tn(a, b):
    return lax.dot_general(a, b, (((0,), (0,)), ((), ())), preferred_element_type=F32)


def _split_hi_lo(a):
    hi = a.astype(BF16)
    lo = (a - hi.astype(F32)).astype(BF16)
    return hi, lo


def _silu(x):
    return x * (1.0 / (1.0 + jnp.exp(-x)))


def _sigmoid(x):
    return 1.0 / (1.0 + jnp.exp(-x))


def _log_sigmoid(x):
    return jnp.minimum(x, 0.0) - jnp.log(1.0 + jnp.exp(-jnp.abs(x)))


def _rms_rows(x, g):
    ms = jnp.mean(x * x, axis=-1, keepdims=True)
    return x * lax.rsqrt(ms + RMS_EPS) * g


def _params(*sem):
    return pltpu.CompilerParams(dimension_semantics=sem, vmem_limit_bytes=VMEM_LIMIT)


def _resident(shape):
    nd = len(shape)
    return pl.BlockSpec(shape, lambda *_: (0,) * nd, pipeline_mode=pl.Buffered(1))


IN_SEGS = (("q_a", NSA_WIDTH), ("kv", 6 * NSA_KV_WIDTH), ("qk_b", 2 * M_WIDTH),
           ("v_b", M_WIDTH), ("o_b", M_WIDTH), ("small", SMALL_W))
IN_COLS = sum(w for _, w in IN_SEGS)


def _in_proj_kernel(x_ref, g_ref, w_ref, b_ref, *out_refs):
    x = x_ref[...]
    h = _rms_rows(x, g_ref[...]).astype(BF16)
    c0 = 0
    for o_ref, (_, width) in zip(out_refs, IN_SEGS):
        o_ref[...] = _dot(h, w_ref[:, c0:c0 + width]) + b_ref[:, c0:c0 + width]
        c0 += width


def _permute_in_proj(w_in, b_in):
    o = 0
    parts = {}
    for name, width in (("q_a", NSA_WIDTH), ("kv", 6 * NSA_KV_WIDTH), ("g_a", 3 * NSA_HEADS),
                        ("q_b", M_WIDTH), ("k_b", M_WIDTH), ("v_b", M_WIDTH),
                        ("i_b", M_HEADS), ("f_b", M_HEADS), ("o_b", M_WIDTH)):
        parts[name] = (o, o + width)
        o += width

    def perm(a):
        def sl(name):
            lo, hi = parts[name]
            return a[..., lo:hi]
        lead = a.shape[:-1]
        g_a = sl("g_a")
        small0 = jnp.concatenate([g_a[..., :GATE_COLS],
                                  jnp.zeros(lead + (LANES - GATE_COLS,), a.dtype)], axis=-1)
        small1 = jnp.concatenate([g_a[..., GATE_COLS:], sl("i_b"), sl("f_b"),
                                  jnp.zeros(lead + (LANES - GATE_COLS - 2 * M_HEADS,), a.dtype)], axis=-1)
        return jnp.concatenate([sl("q_a"), sl("kv"), sl("q_b"), sl("k_b"), sl("v_b"), sl("o_b"),
                                small0, small1], axis=-1)

    return perm(w_in).astype(BF16), perm(b_in)[None, :]


def _in_proj(x2, ln_g, w_perm, b_perm):
    n = x2.shape[0]
    tm = min(PROJ_TM, n)
    out_shape = [jax.ShapeDtypeStruct((n, w), F32) for _, w in IN_SEGS]
    out_specs = [pl.BlockSpec((tm, w), lambda i: (i, 0)) for _, w in IN_SEGS]
    return pl.pallas_call(
        _in_proj_kernel,
        grid=(n // tm,),
        in_specs=[pl.BlockSpec((tm, D_MODEL), lambda i: (i, 0)),
                  _resident((1, D_MODEL)),
                  _resident((D_MODEL, IN_COLS)),
                  _resident((1, IN_COLS))],
        out_specs=out_specs,
        out_shape=out_shape,
        compiler_params=_params("parallel"),
        name="in_proj",
    )(x2, ln_g[None, :], w_perm, b_perm)


def _cmp_kernel(k_ref, v_ref, pe_ref, w1_ref, w2k_ref, w2vt_ref, kg_ref, kc_ref, vct_ref, *, n_cmp):
    acc = [[jnp.zeros((n_cmp, 2 * CMP_HIDDEN), F32) for _ in range(2)] for _ in range(2)]
    for l in range(CMP_STRIDE):
        for w, x_ref in enumerate((k_ref, v_ref)):
            xw = x_ref[pl.ds(l, n_cmp, stride=CMP_STRIDE), :]
            for half in range(2):
                pos = half * CMP_STRIDE + l
                acc[w][half] = acc[w][half] + _dot((xw + pe_ref[w, pos:pos + 1, :]).astype(BF16),
                                                   w1_ref[w, pos])
    act = [_silu(acc[w][0] + pltpu.roll(acc[w][1], n_cmp - 1, 0)).astype(BF16) for w in range(2)]
    k_out = _dot(act[0], w2k_ref[...])
    k_out = jnp.concatenate([_rms_rows(k_out[:, g * NSA_DH:(g + 1) * NSA_DH], kg_ref[...])
                             for g in range(NSA_KV_GROUPS)], axis=1)
    kc_ref[...] = k_out.astype(BF16)
    vct_ref[...] = _dot_nt(w2vt_ref[...], act[1]).astype(BF16)


def _block_diag2(w):
    z = jnp.zeros_like(w)
    return jnp.concatenate([jnp.concatenate([w, z], axis=-1), jnp.concatenate([z, w], axis=-1)], axis=-2)


def _compress(kv, pe, w1, w2, kg, batch, seq_len):
    n_cmp = seq_len // CMP_STRIDE
    dh, gw = NSA_DH, NSA_KV_WIDTH
    pe2 = jnp.concatenate([pe, pe], axis=-1)
    w1b = _block_diag2(w1.reshape(2, CMP_BLOCK, dh, CMP_HIDDEN)).astype(BF16)
    w2b = _block_diag2(w2).astype(BF16)
    w2k, w2v_t = w2b[0], w2b[1].T
    return pl.pallas_call(
        functools.partial(_cmp_kernel, n_cmp=n_cmp),
        grid=(batch,),
        in_specs=[pl.BlockSpec((seq_len, gw), lambda b: (b, 0)),
                  pl.BlockSpec((seq_len, gw), lambda b: (b, 1)),
                  pl.BlockSpec((2, CMP_BLOCK, gw), lambda b: (0, 0, 0)),
                  pl.BlockSpec((2, CMP_BLOCK, gw, 2 * CMP_HIDDEN), lambda b: (0, 0, 0, 0)),
                  pl.BlockSpec((2 * CMP_HIDDEN, gw), lambda b: (0, 0)),
                  pl.BlockSpec((gw, 2 * CMP_HIDDEN), lambda b: (0, 0)),
                  pl.BlockSpec((1, dh), lambda b: (0, 0))],
        out_specs=[pl.BlockSpec((None, n_cmp, gw), lambda b: (b, 0, 0)),
                   pl.BlockSpec((None, gw, n_cmp), lambda b: (b, 0, 0))],
        out_shape=[jax.ShapeDtypeStruct((batch, n_cmp, gw), BF16),
                   jax.ShapeDtypeStruct((batch, gw, n_cmp), BF16)],
        compiler_params=_params("parallel"),
        name="cmp_kv",
    )(kv, kv, pe2, w1b, w2k, w2v_t, kg)


def _attn_prep_kernel(q_ref, sel_ref, win_ref, qg_ref, kg_ref, qt_ref, ksa_ref, kw_ref, vst_ref, vwt_ref):
    tm = q_ref.shape[0]
    dh, gw, hg = NSA_DH, NSA_KV_WIDTH, NSA_HG
    t_base = pl.program_id(1) * tm
    sel, win = sel_ref[...], win_ref[...]
    seg_r = lax.broadcasted_iota(jnp.int32, (gw, gw), 0) // dh
    seg_c = lax.broadcasted_iota(jnp.int32, (gw, gw), 1) // dh
    seg_ones = jnp.where(seg_r == seg_c, 1.0, 0.0).astype(BF16)

    def head_norm(x, g):
        hi, lo = _split_hi_lo(x * x)
        ss = _dot(hi, seg_ones) + _dot(lo, seg_ones)
        return (x * lax.rsqrt(ss * (1.0 / dh) + RMS_EPS) * g).astype(BF16)

    is_dummy = pl.program_id(1) == pl.num_programs(1) - 1
    key = t_base + lax.broadcasted_iota(jnp.int32, (tm, gw), 0)
    col = lax.broadcasted_iota(jnp.int32, (tm, gw), 1)
    onehot = jnp.where(col == jnp.where(is_dummy, DEAD_COL, key // SLC_BLOCK), 1.0, 0.0).astype(BF16)
    k_sel = head_norm(sel[:, 0:gw], kg_ref[0:1, :])
    ksa_ref[...] = jnp.concatenate([jnp.where(is_dummy, jnp.zeros_like(k_sel), k_sel), onehot], axis=1)
    kw_ref[...] = head_norm(win[:, 0:gw], kg_ref[1:2, :])

    for j in range(tm // LANES):
        r = slice(j * LANES, (j + 1) * LANES)
        vst_ref[j] = sel[r, gw:].T.astype(BF16)
        vwt_ref[j] = win[r, gw:].T.astype(BF16)

    sub = ATT_TQ // LANES
    zeros = jnp.zeros((dh, hg * ATT_TQ), BF16)
    for j in range(tm // ATT_TQ):
        x_t = [q_ref[(j * sub + u) * LANES:(j * sub + u + 1) * LANES, :].T for u in range(sub)]
        for g in range(NSA_KV_GROUPS):
            heads = []
            for h in range(hg):
                for u in range(sub):
                    seg = x_t[u][(g * hg + h) * dh:(g * hg + h + 1) * dh, :]
                    ms = jnp.mean(seg * seg, axis=0, keepdims=True)
                    heads.append(seg * lax.rsqrt(ms + RMS_EPS) * qg_ref[...] * (dh ** -0.5 * LOG2E))
            blockq = jnp.concatenate(heads, axis=1).astype(BF16)
            qt_ref[j, g] = jnp.concatenate([blockq, zeros] if g == 0 else [zeros, blockq], axis=0)


def _attn_prep(q_a, kv, q_g, k_g, batch, seq_len):
    tm = SEL_BODY
    nt = seq_len // tm
    gw, hg = NSA_KV_WIDTH, NSA_HG
    n_lt = seq_len // LANES
    qg_b = jnp.broadcast_to(q_g[:, None], (NSA_DH, LANES))
    kg_b = jnp.concatenate([k_g[1:3], k_g[1:3]], axis=1)
    last = lambda c: jnp.minimum(c, nt - 1)
    return pl.pallas_call(
        _attn_prep_kernel,
        grid=(batch, nt + 1),
        in_specs=[pl.BlockSpec((tm, NSA_WIDTH), lambda b, c: (b * nt + last(c), 0)),
                  pl.BlockSpec((tm, 2 * gw), lambda b, c: (b * nt + last(c), 1)),
                  pl.BlockSpec((tm, 2 * gw), lambda b, c: (b * nt + last(c), 2)),
                  pl.BlockSpec((NSA_DH, LANES), lambda b, c: (0, 0)),
                  pl.BlockSpec((2, gw), lambda b, c: (0, 0))],
        out_specs=[pl.BlockSpec((None, tm // ATT_TQ, NSA_KV_GROUPS, gw, hg * ATT_TQ),
                                lambda b, c: (b, last(c), 0, 0, 0)),
                   pl.BlockSpec((None, tm, 2 * gw), lambda b, c: (b, c, 0)),
                   pl.BlockSpec((None, tm, gw), lambda b, c: (b, last(c), 0)),
                   pl.BlockSpec((None, tm // LANES, gw, LANES), lambda b, c: (b, last(c), 0, 0)),
                   pl.BlockSpec((None, tm // LANES, gw, LANES), lambda b, c: (b, last(c), 0, 0))],
        out_shape=[jax.ShapeDtypeStruct((batch, seq_len // ATT_TQ, NSA_KV_GROUPS, gw, hg * ATT_TQ), BF16),
                   jax.ShapeDtypeStruct((batch, seq_len + tm, 2 * gw), BF16),
                   jax.ShapeDtypeStruct((batch, seq_len, gw), BF16),
                   jax.ShapeDtypeStruct((batch, n_lt, gw, LANES), BF16),
                   jax.ShapeDtypeStruct((batch, n_lt, gw, LANES), BF16)],
        compiler_params=_params("parallel", "parallel"),
        name="attn_prep",
    )(q_a, kv, kv, qg_b, kg_b)


def _softmax_pv(s, v_t):
    m = jnp.max(s, axis=0, keepdims=True)
    p = jnp.exp2(s - m)
    l = jnp.sum(p, axis=0, keepdims=True)
    return _dot(v_t, p.astype(BF16)) * (1.0 / l)


def _nsa_kernel(qt_ref, kc_ref, vct_ref, ksa_ref, vst_ref, kw_ref, vwt_ref, gate_ref, og_ref,
                o_ref, score_ref, qaug_ref, s0, s1, s_last, p0, p1, a0, a1, m_ref, l_ref, acc_ref,
                *, seq_len):
    tq, hg, dh, gw = ATT_TQ, NSA_HG, NSA_DH, NSA_KV_WIDTH
    rows = hg * tq
    n_cmp = seq_len // CMP_STRIDE
    n_blk = seq_len // SLC_BLOCK
    n_sel = min(SLC_TOPN, n_blk)
    body = SEL_BODY
    i = pl.program_id(1)
    t0 = pl.multiple_of(i * tq, tq)
    qt = qt_ref[...]

    def lane_tile(a):
        return jnp.concatenate([a] * hg, axis=1)

    def v_tiles(vt_ref, k0, tk):
        return jnp.concatenate([vt_ref[k0 // LANES + jt] for jt in range(tk // LANES)], axis=1)

    def fold8(a):
        return a.reshape(a.shape[0] // SUBLANES, SUBLANES, rows)

    span = WINDOW + tq
    w0 = pl.multiple_of(jnp.clip(t0 - WINDOW, 0, seq_len - span), tq)
    k_pos = w0 + lax.broadcasted_iota(jnp.int32, (span, tq), 0)
    q_pos = t0 + lax.broadcasted_iota(jnp.int32, (span, tq), 1)
    w_bias = jnp.where(k_pos <= q_pos, jnp.where(k_pos > q_pos - WINDOW, 0.0, NEG_INF), NEG_INF)
    s_w = _dot(kw_ref[pl.ds(w0, span), :], qt) + lane_tile(w_bias)
    o_win = _softmax_pv(s_w, v_tiles(vwt_ref, w0, span))

    c_end = lax.broadcasted_iota(jnp.int32, (n_cmp, tq), 0) * CMP_STRIDE + (CMP_BLOCK - 1)
    c_ok = c_end <= t0 + lax.broadcasted_iota(jnp.int32, (n_cmp, tq), 1)
    s_c = _dot(kc_ref[...], qt) + lane_tile(jnp.where(c_ok, 0.0, NEG_INF))
    m_c = jnp.max(s_c, axis=0, keepdims=True)
    e_c = jnp.exp2(s_c - m_c) * lane_tile(jnp.where(c_ok, 1.0, 0.0))
    l_c = jnp.sum(e_c, axis=0, keepdims=True)
    p_c = e_c * (1.0 / jnp.where(l_c > 0.0, l_c, 1.0))
    o_cmp = _dot(vct_ref[...], p_c.astype(BF16))

    p_sum = p_c[:, 0:tq]
    for h in range(1, hg):
        p_sum = p_sum + p_c[:, h * tq:(h + 1) * tq]
    blk_r = lax.broadcasted_iota(jnp.int32, (n_blk, n_cmp), 0)
    cmp_c = lax.broadcasted_iota(jnp.int32, (n_blk, n_cmp), 1)
    overlap = ((cmp_c * CMP_STRIDE < (blk_r + 1) * SLC_BLOCK)
               & (cmp_c * CMP_STRIDE + (CMP_BLOCK - 1) >= blk_r * SLC_BLOCK)
               & (cmp_c < n_cmp - 1))
    overlap = jnp.where(overlap, 1.0, 0.0).astype(BF16)
    p_hi, p_lo = _split_hi_lo(p_sum)
    imp = _dot(overlap, p_hi) + _dot(overlap, p_lo)

    blk = lax.broadcasted_iota(jnp.int32, (n_blk, tq), 0)
    t_col = t0 + lax.broadcasted_iota(jnp.int32, (n_blk, tq), 1)
    cur = t_col // SLC_BLOCK
    forced = (blk == 0) | (blk == cur) | (blk == cur - 1)
    score = jnp.where(blk * SLC_BLOCK <= t_col,
                      imp + jnp.where(forced, FORCE_BONUS, 0.0), NEG_INF)
    score_ref[...] = score

    def rank_body(jp, rank):
        row = score_ref[pl.ds(jp, 1), :]
        return rank + jnp.where(blk > jp, jnp.where(row >= score, 1.0, 0.0),
                                jnp.where(row > score, 1.0, 0.0))

    n_live = jnp.minimum((t0 + tq) // SLC_BLOCK, n_blk)
    rank = lax.fori_loop(0, n_live, rank_body, jnp.zeros((n_blk, tq), F32))
    neg = jnp.where(rank < float(n_sel), 0.0, NEG_INF)

    dead = lax.broadcasted_iota(jnp.int32, (gw - n_blk, rows), 0) == DEAD_COL - n_blk
    qaug_ref[0:gw, :] = qt
    qaug_ref[gw:gw + n_blk, :] = lane_tile(neg).astype(BF16)
    qaug_ref[gw + n_blk:2 * gw, :] = jnp.where(dead, NEG_INF, 0.0).astype(BF16)
    q_aug = qaug_ref[...]

    n_full = t0 // body
    k_last = pl.multiple_of(n_full * body, body)

    def scores(n):
        k0 = pl.multiple_of(jnp.where(n < n_full, n * body, seq_len), body)
        return _dot(ksa_ref[pl.ds(k0, body), :], q_aug)

    def pv(n, p_ref, a_ref):
        k0 = pl.multiple_of(jnp.clip(n, 0, seq_len // body - 1) * body, body)
        acc_ref[...] = a_ref[...] * acc_ref[...] + _dot(v_tiles(vst_ref, k0, body), p_ref[...])

    def softmax(s_ref, mx8, p_ref, a_ref):
        m_old = m_ref[...]
        m_new = jnp.maximum(m_old, jnp.max(mx8, axis=0, keepdims=True))
        alpha = jnp.exp2(m_old - m_new)
        p = jnp.exp2(s_ref[...] - m_new)
        l_ref[...] = alpha * l_ref[...] + jnp.sum(fold8(p), axis=0)
        m_ref[...] = m_new
        a_ref[...] = alpha
        p_ref[...] = p.astype(BF16)

    def stage(n, s_cur, mx_cur, p_prev, a_prev, p_cur, a_cur, s_next):
        pv(n - 1, p_prev, a_prev)
        softmax(s_cur, mx_cur, p_cur, a_cur)
        s = scores(n + 1)
        s_next[...] = s
        return jnp.max(fold8(s), axis=0)

    m_ref[...] = jnp.full((1, rows), NEG_INF, F32)
    l_ref[...] = jnp.zeros((SUBLANES, rows), F32)
    acc_ref[...] = jnp.zeros((dh, rows), F32)
    p1[...] = jnp.zeros((body, rows), BF16)
    a1[...] = jnp.ones((1, rows), F32)
    k_b = k_last + lax.broadcasted_iota(jnp.int32, (body, tq), 0)
    q_b = t0 + lax.broadcasted_iota(jnp.int32, (body, tq), 1)
    s = _dot(ksa_ref[pl.ds(k_last, body), :], q_aug) + lane_tile(jnp.where(k_b <= q_b, 0.0, NEG_INF))
    s_last[...] = s
    mx_last = jnp.max(fold8(s), axis=0)
    s = scores(0)
    s0[...] = s

    def sel_body(j, mx0):
        mx1 = stage(2 * j, s0, mx0, p1, a1, p0, a0, s1)
        return stage(2 * j + 1, s1, mx1, p0, a0, p1, a1, s0)

    n_trips = (n_full + 1) // 2
    lax.fori_loop(0, n_trips, sel_body, jnp.max(fold8(s), axis=0))
    pv(2 * n_trips - 1, p1, a1)
    softmax(s_last, mx_last, p0, a0)
    pv(n_full, p0, a0)
    o_sel = acc_ref[...] * (1.0 / jnp.sum(l_ref[...], axis=0, keepdims=True))

    gate_t = _sigmoid(gate_ref[...]).T
    outs = []
    for h in range(hg):
        hs = slice(h * tq, (h + 1) * tq)
        o = (gate_t[3 * h:3 * h + 1] * o_cmp[:, hs] + gate_t[3 * h + 1:3 * h + 2] * o_sel[:, hs]
             + gate_t[3 * h + 2:3 * h + 3] * o_win[:, hs])
        ms = jnp.mean(o * o, axis=0, keepdims=True)
        outs.append(o * lax.rsqrt(ms + RMS_EPS) * og_ref[h * dh:(h + 1) * dh, :])
    o_ref[...] = jnp.concatenate(outs, axis=0).T


def _nsa_attn(q_t, k_cmp, v_cmp_t, ks_aug, vs_t, kw, vw_t, small, out_g_b, batch, seq_len):
    g, hg, dh, tq, gw = NSA_KV_GROUPS, NSA_HG, NSA_DH, ATT_TQ, NSA_KV_WIDTH
    nq = seq_len // tq
    n_cmp = seq_len // CMP_STRIDE
    n_blk = seq_len // SLC_BLOCK
    n_lt = seq_len // LANES
    rows = hg * tq
    assert n_blk <= DEAD_COL and tq % LANES == 0 and SEL_BODY % tq == 0 and seq_len % SEL_BODY == 0
    vt_spec = pl.BlockSpec((None, n_lt, dh, LANES), lambda b, i, j: (b, 0, j, 0))
    return pl.pallas_call(
        functools.partial(_nsa_kernel, seq_len=seq_len),
        grid=(batch, nq, g),
        in_specs=[pl.BlockSpec((None, None, None, gw, rows), lambda b, i, j: (b, i, j, 0, 0)),
                  pl.BlockSpec((None, n_cmp, gw), lambda b, i, j: (b, 0, 0)),
                  pl.BlockSpec((None, dh, n_cmp), lambda b, i, j: (b, j, 0)),
                  pl.BlockSpec((None, seq_len + SEL_BODY, 2 * gw), lambda b, i, j: (b, 0, 0)),
                  vt_spec,
                  pl.BlockSpec((None, seq_len, gw), lambda b, i, j: (b, 0, 0)),
                  vt_spec,
                  pl.BlockSpec((tq, LANES), lambda b, i, j: (b * nq + i, j)),
                  pl.BlockSpec((None, hg * dh, tq), lambda b, i, j: (j, 0, 0))],
        out_specs=pl.BlockSpec((tq, hg * dh), lambda b, i, j: (b * nq + i, j)),
        out_shape=jax.ShapeDtypeStruct((batch * seq_len, NSA_WIDTH), F32),
        scratch_shapes=[pltpu.VMEM((n_blk, tq), F32),
                        pltpu.VMEM((2 * gw, rows), BF16),
                        pltpu.VMEM((SEL_BODY, rows), F32),
                        pltpu.VMEM((SEL_BODY, rows), F32),
                        pltpu.VMEM((SEL_BODY, rows), F32),
                        pltpu.VMEM((SEL_BODY, rows), BF16),
                        pltpu.VMEM((SEL_BODY, rows), BF16),
                        pltpu.VMEM((1, rows), F32),
                        pltpu.VMEM((1, rows), F32),
                        pltpu.VMEM((1, rows), F32),
                        pltpu.VMEM((SUBLANES, rows), F32),
                        pltpu.VMEM((dh, rows), F32)],
        compiler_params=_params("parallel", "arbitrary", "arbitrary"),
        name="nsa_attn",
    )(q_t, k_cmp, v_cmp_t, ks_aug, vs_t, kw, vw_t, small, out_g_b)


def _mlstm_kernel(qk_ref, v_ref, o_ref, small_ref, ift_ref, cw_ref, cb_ref, fb_ref, fbt_ref, og_ref,
                  y_ref, xbuf, c_st, n_st, m_st):
    lc = M_CHUNK
    nh, dh, width = M_HEADS, M_DH, M_WIDTH
    c = pl.program_id(1)

    @pl.when(c == 0)
    def _():
        xbuf[0:SUBLANES, :] = jnp.zeros((SUBLANES, 2 * width), F32)
        c_st[...] = jnp.zeros(c_st.shape, F32)
        n_st[...] = jnp.zeros(n_st.shape, F32)
        m_st[...] = jnp.zeros(m_st.shape, F32)

    u = qk_ref[...]
    xbuf[SUBLANES:SUBLANES + lc, :] = u
    y = cb_ref[...] + cw_ref[0:1, :] * xbuf[pl.ds(SUBLANES - (CONV_WIDTH - 1), lc), :]
    for j in range(1, CONV_WIDTH):
        y = y + cw_ref[j:j + 1, :] * xbuf[pl.ds(SUBLANES - (CONV_WIDTH - 1) + j, lc), :]
    xbuf[0:SUBLANES, :] = u[lc - SUBLANES:lc, :]
    qk = _silu(y)

    r_i = lax.broadcasted_iota(jnp.int32, (lc, lc), 0)
    c_i = lax.broadcasted_iota(jnp.int32, (lc, lc), 1)
    causal = c_i <= r_i
    tril = jnp.where(causal, 1.0, 0.0).astype(BF16)
    triu = jnp.where(r_i <= c_i, 1.0, 0.0).astype(BF16)
    small = small_ref[...]
    lf_hi, lf_lo = _split_hi_lo(_log_sigmoid(small + fb_ref[...]))
    b_col_all = _dot(tril, lf_hi) + _dot(tril, lf_lo)
    ift = ift_ref[...]
    lft_hi, lft_lo = _split_hi_lo(_log_sigmoid(ift + fbt_ref[:, 0:1]))
    b_row_all = _dot(lft_hi, triu) + _dot(lft_lo, triu)

    v_all = v_ref[...]
    og_all = _sigmoid(o_ref[...])
    for h in range(nh):
        hs = slice(h * dh, (h + 1) * dh)
        q_h = qk[:, h * dh:(h + 1) * dh]
        k_h = qk[:, width + h * dh:width + (h + 1) * dh] * (dh ** -0.5)
        v_h = v_all[:, hs]
        q_b, k_b, v_b = q_h.astype(BF16), k_h.astype(BF16), v_h.astype(BF16)
        b_col = b_col_all[:, F_COL + h:F_COL + h + 1]
        i_col = small[:, I_COL + h:I_COL + h + 1]
        b_row = b_row_all[nh + h:nh + h + 1, :]
        i_row = ift[h:h + 1, :]
        m_prev = m_st[h:h + 1, 0:1]
        c_prev = c_st[h]
        n_prev = n_st[h:h + 1, :]

        a = b_col + m_prev
        d = jnp.where(causal, b_col + (i_row - b_row), NEG_INF)
        m_t = jnp.maximum(a, jnp.max(d, axis=-1, keepdims=True))
        w_inter = jnp.exp(a - m_t)
        w_intra = jnp.exp(d - m_t) * _dot_nt(q_b, k_b)
        num = w_inter * _dot(q_b, c_prev.astype(BF16)) + _dot(w_intra.astype(BF16), v_b)
        den = (w_inter * jnp.sum(q_h * n_prev, axis=-1, keepdims=True)
               + jnp.sum(w_intra, axis=-1, keepdims=True))
        hid = num * (1.0 / jnp.maximum(jnp.abs(den), jnp.exp(-m_t)))

        b_last = b_col[lc - 1:lc, :]
        g_col = b_last - b_col + i_col
        m_new = jnp.maximum(b_last + m_prev, jnp.max(g_col, axis=0, keepdims=True))
        decay = jnp.exp(b_last + m_prev - m_new)
        w_s = jnp.exp(g_col - m_new)
        c_st[h] = decay * c_prev + _dot_tn(k_b, (w_s * v_h).astype(BF16))
        n_st[h:h + 1, :] = decay * n_prev + jnp.sum(w_s * k_h, axis=0, keepdims=True)
        m_st[h:h + 1, :] = jnp.broadcast_to(m_new, (1, LANES))

        y_ref[:, hs] = og_all[:, hs] * _rms_rows(hid, og_ref[:, hs])


def _mlstm(qk_b, v_b, o_b, small, if_t, conv_w, conv_b, fb_col, fb_row, out_g, batch, seq_len):
    n = qk_b.shape[0]
    lc = M_CHUNK
    nc = seq_len // lc
    row = lambda b, c: (b * nc + c, 0)
    return pl.pallas_call(
        _mlstm_kernel,
        grid=(batch, nc),
        in_specs=[pl.BlockSpec((lc, 2 * M_WIDTH), row),
                  pl.BlockSpec((lc, M_WIDTH), row),
                  pl.BlockSpec((lc, M_WIDTH), row),
                  pl.BlockSpec((lc, LANES), lambda b, c: (b * nc + c, 1)),
                  pl.BlockSpec((None, 2 * M_HEADS, lc), lambda b, c: (b, 0, c)),
                  pl.BlockSpec((CONV_WIDTH, 2 * M_WIDTH), lambda b, c: (0, 0)),
                  pl.BlockSpec((1, 2 * M_WIDTH), lambda b, c: (0, 0)),
                  pl.BlockSpec((1, LANES), lambda b, c: (0, 0)),
                  pl.BlockSpec((2 * M_HEADS, LANES), lambda b, c: (0, 0)),
                  pl.BlockSpec((1, M_WIDTH), lambda b, c: (0, 0))],
        out_specs=pl.BlockSpec((lc, M_WIDTH), row),
        out_shape=jax.ShapeDtypeStruct((n, M_WIDTH), F32),
        scratch_shapes=[pltpu.VMEM((lc + SUBLANES, 2 * M_WIDTH), F32),
                        pltpu.VMEM((M_HEADS, M_DH, M_DH), F32),
                        pltpu.VMEM((SUBLANES, M_DH), F32),
                        pltpu.VMEM((SUBLANES, LANES), F32)],
        compiler_params=_params("parallel", "arbitrary"),
        name="mlstm",
    )(qk_b, v_b, o_b, small, if_t, conv_w, conv_b, fb_col, fb_row, out_g)


def _out_ffn_kernel(x_ref, ya_ref, yb_ref, wo_ref, g_ref, wgu_ref, wd_ref, o_ref):
    x1 = (x_ref[...] + _dot(ya_ref[...].astype(BF16), wo_ref[0:NSA_WIDTH, :])
          + _dot(yb_ref[...].astype(BF16), wo_ref[NSA_WIDTH:, :]))
    h = _rms_rows(x1, g_ref[...]).astype(BF16)
    o_ref[...] = x1
    for c in range(FFN_HIDDEN // FFN_FC):
        cs = slice(c * FFN_FC, (c + 1) * FFN_FC)
        us = slice(FFN_HIDDEN + c * FFN_FC, FFN_HIDDEN + (c + 1) * FFN_FC)
        act = (_silu(_dot(h, wgu_ref[:, cs])) * _dot(h, wgu_ref[:, us])).astype(BF16)
        o_ref[...] += _dot(act, wd_ref[cs, :])


def _out_ffn(x2, y_a, y_b, w_out, ln_g, w_gu, w_down):
    n = x2.shape[0]
    tm = min(PROJ_TM, n)
    row = lambda i: (i, 0)
    return pl.pallas_call(
        _out_ffn_kernel,
        grid=(n // tm,),
        in_specs=[pl.BlockSpec((tm, D_MODEL), row),
                  pl.BlockSpec((tm, NSA_WIDTH), row),
                  pl.BlockSpec((tm, M_WIDTH), row),
                  _resident((D_MODEL, D_MODEL)),
                  _resident((1, D_MODEL)),
                  _resident((D_MODEL, 2 * FFN_HIDDEN)),
                  _resident((FFN_HIDDEN, D_MODEL))],
        out_specs=pl.BlockSpec((tm, D_MODEL), row),
        out_shape=jax.ShapeDtypeStruct((n, D_MODEL), F32),
        compiler_params=_params("parallel"),
        name="out_ffn",
    )(x2, y_a, y_b, w_out, ln_g[None, :], w_gu, w_down)


def _layer(x2, batch, seq_len, ln1_g, w_in, b_in, q_g, k_g, pe, w1, w2, conv_w, conv_b, fgate_b,
           nsa_out_g, m_out_g, w_out, ln2_g, w_gu, w_down):
    g, dh = NSA_KV_GROUPS, NSA_DH
    w_perm, b_perm = _permute_in_proj(w_in, b_in)
    q_a, kv, qk_b, v_b, o_b, small = _in_proj(x2, ln1_g, w_perm, b_perm)

    k_cmp, v_cmp_t = _compress(kv, pe, w1, w2, k_g[0:1], batch, seq_len)
    q_t, ks_aug, kw, vs_t, vw_t = _attn_prep(q_a, kv, q_g, k_g, batch, seq_len)
    out_g_b = jnp.broadcast_to(nsa_out_g.reshape(g, NSA_HG * dh, 1), (g, NSA_HG * dh, ATT_TQ))
    y_a = _nsa_attn(q_t, k_cmp, v_cmp_t, ks_aug, vs_t, kw, vw_t, small, out_g_b, batch, seq_len)

    if_t = small[:, LANES + I_COL:LANES + I_COL + 2 * M_HEADS].reshape(
        batch, seq_len, 2 * M_HEADS).transpose(0, 2, 1)
    fb_col = jnp.zeros((1, LANES), F32).at[0, F_COL:F_COL + M_HEADS].set(fgate_b)
    fb_row = jnp.zeros((2 * M_HEADS, LANES), F32).at[M_HEADS:, :].set(
        jnp.broadcast_to(fgate_b[:, None], (M_HEADS, LANES)))
    y_b = _mlstm(qk_b, v_b, o_b, small, if_t, conv_w, conv_b[None, :], fb_col, fb_row,
                 m_out_g[None, :], batch, seq_len)

    return _out_ffn(x2, y_a, y_b, w_out.astype(BF16), ln2_g, w_gu.astype(BF16), w_down.astype(BF16))


def kernel(x, ln1_g, w_in, b_in, nsa_q_norm_g, nsa_k_norm_g, cmp_pe, cmp_w1, cmp_w2, m_conv_w,
           m_conv_b, m_fgate_b, nsa_out_norm_g, m_out_norm_g, w_out, ln2_g, w_gate_up, w_down):
    batch, seq_len, d = x.shape
    x2 = x.reshape(batch * seq_len, d)
    for l in range(ln1_g.shape[0]):
        x2 = _layer(x2, batch, seq_len, ln1_g[l], w_in[l], b_in[l], nsa_q_norm_g[l], nsa_k_norm_g[l],
                    cmp_pe[l], cmp_w1[l], cmp_w2[l], m_conv_w[l], m_conv_b[l], m_fgate_b[l],
                    nsa_out_norm_g[l], m_out_norm_g[l], w_out[l], ln2_g[l], w_gate_up[l], w_down[l])
    return x2.reshape(batch, seq_len, d)
```

```python
import functools

import jax
import jax.numpy as jnp
from jax import lax
from jax.experimental import pallas as pl
from jax.experimental.pallas import tpu as pltpu

F32 = jnp.float32
BF16 = jnp.bfloat16

D_MODEL = 1024
NSA_HEADS = 8
NSA_KV_GROUPS = 2
NSA_HG = NSA_HEADS // NSA_KV_GROUPS
NSA_DH = 64
NSA_WIDTH = NSA_HEADS * NSA_DH
NSA_KV_WIDTH = NSA_KV_GROUPS * NSA_DH
CMP_BLOCK = 32
CMP_STRIDE = 16
CMP_HIDDEN = 2 * NSA_DH
SLC_BLOCK = 64
SLC_TOPN = 16
WINDOW = 512
FORCE_BONUS = 1e4
M_HEADS = 4
M_WIDTH = 512
M_DH = 128
CONV_WIDTH = 4
FFN_HIDDEN = 2816
RMS_EPS = 1e-6
NEG_INF = -1e30
LOG2E = 1.4426950408889634

LANES = 128
SUBLANES = 8
VMEM_LIMIT = 56 * 1024 * 1024

PROJ_TM = 512
ATT_TQ = 256
SEL_BODY = 256
DEAD_COL = 64
M_CHUNK = 256
FFN_FC = 256

SMALL_W = 2 * LANES
GATE_COLS = 3 * NSA_HG
I_COL = GATE_COLS
F_COL = GATE_COLS + M_HEADS


def _dot(a, b):
    return jnp.dot(a, b, preferred_element_type=F32)


def _dot_nt(a, b):
    return lax.dot_general(a, b, (((1,), (1,)), ((), ())), preferred_element_type=F32)


def _dot_tn(a, b):
    return lax.dot_general(a, b, (((0,), (0,)), ((), ())), preferred_element_type=F32)


def _split_hi_lo(a):
    hi = a.astype(BF16)
    lo = (a - hi.astype(F32)).astype(BF16)
    return hi, lo


def _silu(x):
    return x * (1.0 / (1.0 + jnp.exp(-x)))


def _sigmoid(x):
    return 1.0 / (1.0 + jnp.exp(-x))


def _log_sigmoid(x):
    return jnp.minimum(x, 0.0) - jnp.log(1.0 + jnp.exp(-jnp.abs(x)))


def _rms_rows(x, g):
    ms = jnp.mean(x * x, axis=-1, keepdims=True)
    return x * lax.rsqrt(ms + RMS_EPS) * g


def _params(*sem):
    return pltpu.CompilerParams(dimension_semantics=sem, vmem_limit_bytes=VMEM_LIMIT)


def _resident(shape):
    nd = len(shape)
    return pl.BlockSpec(shape, lambda *_: (0,) * nd, pipeline_mode=pl.Buffered(1))


IN_SEGS = (("q_a", NSA_WIDTH), ("kv", 6 * NSA_KV_WIDTH), ("qk_b", 2 * M_WIDTH),
           ("v_b", M_WIDTH), ("o_b", M_WIDTH), ("small", SMALL_W))
IN_COLS = sum(w for _, w in IN_SEGS)


def _in_proj_kernel(x_ref, g_ref, w_ref, b_ref, *out_refs):
    x = x_ref[...]
    h = _rms_rows(x, g_ref[...]).astype(BF16)
    c0 = 0
    for o_ref, (_, width) in zip(out_refs, IN_SEGS):
        o_ref[...] = _dot(h, w_ref[:, c0:c0 + width]) + b_ref[:, c0:c0 + width]
        c0 += width


def _permute_in_proj(w_in, b_in):
    o = 0
    parts = {}
    for name, width in (("q_a", NSA_WIDTH), ("kv", 6 * NSA_KV_WIDTH), ("g_a", 3 * NSA_HEADS),
                        ("q_b", M_WIDTH), ("k_b", M_WIDTH), ("v_b", M_WIDTH),
                        ("i_b", M_HEADS), ("f_b", M_HEADS), ("o_b", M_WIDTH)):
        parts[name] = (o, o + width)
        o += width

    def perm(a):
        def sl(name):
            lo, hi = parts[name]
            return a[..., lo:hi]
        lead = a.shape[:-1]
        g_a = sl("g_a")
        small0 = jnp.concatenate([g_a[..., :GATE_COLS],
                                  jnp.zeros(lead + (LANES - GATE_COLS,), a.dtype)], axis=-1)
        small1 = jnp.concatenate([g_a[..., GATE_COLS:], sl("i_b"), sl("f_b"),
                                  jnp.zeros(lead + (LANES - GATE_COLS - 2 * M_HEADS,), a.dtype)], axis=-1)
        return jnp.concatenate([sl("q_a"), sl("kv"), sl("q_b"), sl("k_b"), sl("v_b"), sl("o_b"),
                                small0, small1], axis=-1)

    return perm(w_in).astype(BF16), perm(b_in)[None, :]


def _in_proj(x2, ln_g, w_perm, b_perm):
    n = x2.shape[0]
    tm = min(PROJ_TM, n)
    out_shape = [jax.ShapeDtypeStruct((n, w), F32) for _, w in IN_SEGS]
    out_specs = [pl.BlockSpec((tm, w), lambda i: (i, 0)) for _, w in IN_SEGS]
    return pl.pallas_call(
        _in_proj_kernel,
        grid=(n // tm,),
        in_specs=[pl.BlockSpec((tm, D_MODEL), lambda i: (i, 0)),
                  _resident((1, D_MODEL)),
                  _resident((D_MODEL, IN_COLS)),
                  _resident((1, IN_COLS))],
        out_specs=out_specs,
        out_shape=out_shape,
        compiler_params=_params("parallel"),
        name="in_proj",
    )(x2, ln_g[None, :], w_perm, b_perm)


def _cmp_kernel(k_ref, v_ref, pe_ref, w1_ref, w2k_ref, w2vt_ref, kg_ref, kc_ref, vct_ref, *, n_cmp):
    acc = [[jnp.zeros((n_cmp, 2 * CMP_HIDDEN), F32) for _ in range(2)] for _ in range(2)]
    for l in range(CMP_STRIDE):
        for w, x_ref in enumerate((k_ref, v_ref)):
            xw = x_ref[pl.ds(l, n_cmp, stride=CMP_STRIDE), :]
            for half in range(2):
                pos = half * CMP_STRIDE + l
                acc[w][half] = acc[w][half] + _dot((xw + pe_ref[w, pos:pos + 1, :]).astype(BF16),
                                                   w1_ref[w, pos])
    act = [_silu(acc[w][0] + pltpu.roll(acc[w][1], n_cmp - 1, 0)).astype(BF16) for w in range(2)]
    k_out = _dot(act[0], w2k_ref[...])
    k_out = jnp.concatenate([_rms_rows(k_out[:, g * NSA_DH:(g + 1) * NSA_DH], kg_ref[...])
                             for g in range(NSA_KV_GROUPS)], axis=1)
    kc_ref[...] = k_out.astype(BF16)
    vct_ref[...] = _dot_nt(w2vt_ref[...], act[1]).astype(BF16)


def _block_diag2(w):
    z = jnp.zeros_like(w)
    return jnp.concatenate([jnp.concatenate([w, z], axis=-1), jnp.concatenate([z, w], axis=-1)], axis=-2)


def _compress(kv, pe, w1, w2, kg, batch, seq_len):
    n_cmp = seq_len // CMP_STRIDE
    dh, gw = NSA_DH, NSA_KV_WIDTH
    pe2 = jnp.concatenate([pe, pe], axis=-1)
    w1b = _block_diag2(w1.reshape(2, CMP_BLOCK, dh, CMP_HIDDEN)).astype(BF16)
    w2b = _block_diag2(w2).astype(BF16)
    w2k, w2v_t = w2b[0], w2b[1].T
    return pl.pallas_call(
        functools.partial(_cmp_kernel, n_cmp=n_cmp),
        grid=(batch,),
        in_specs=[pl.BlockSpec((seq_len, gw), lambda b: (b, 0)),
                  pl.BlockSpec((seq_len, gw), lambda b: (b, 1)),
                  pl.BlockSpec((2, CMP_BLOCK, gw), lambda b: (0, 0, 0)),
                  pl.BlockSpec((2, CMP_BLOCK, gw, 2 * CMP_HIDDEN), lambda b: (0, 0, 0, 0)),
                  pl.BlockSpec((2 * CMP_HIDDEN, gw), lambda b: (0, 0)),
                  pl.BlockSpec((gw, 2 * CMP_HIDDEN), lambda b: (0, 0)),
                  pl.BlockSpec((1, dh), lambda b: (0, 0))],
        out_specs=[pl.BlockSpec((None, n_cmp, gw), lambda b: (b, 0, 0)),
                   pl.BlockSpec((None, gw, n_cmp), lambda b: (b, 0, 0))],
        out_shape=[jax.ShapeDtypeStruct((batch, n_cmp, gw), BF16),
                   jax.ShapeDtypeStruct((batch, gw, n_cmp), BF16)],
        compiler_params=_params("parallel"),
        name="cmp_kv",
    )(kv, kv, pe2, w1b, w2k, w2v_t, kg)


def _attn_prep_kernel(q_ref, sel_ref, win_ref, qg_ref, kg_ref, qt_ref, ksa_ref, kw_ref, vst_ref, vwt_ref):
    tm = q_ref.shape[0]
    dh, gw, hg = NSA_DH, NSA_KV_WIDTH, NSA_HG
    t_base = pl.program_id(1) * tm
    sel, win = sel_ref[...], win_ref[...]
    seg_r = lax.broadcasted_iota(jnp.int32, (gw, gw), 0) // dh
    seg_c = lax.broadcasted_iota(jnp.int32, (gw, gw), 1) // dh
    seg_ones = jnp.where(seg_r == seg_c, 1.0, 0.0).astype(BF16)

    def head_norm(x, g):
        hi, lo = _split_hi_lo(x * x)
        ss = _dot(hi, seg_ones) + _dot(lo, seg_ones)
        return (x * lax.rsqrt(ss * (1.0 / dh) + RMS_EPS) * g).astype(BF16)

    is_dummy = pl.program_id(1) == pl.num_programs(1) - 1
    key = t_base + lax.broadcasted_iota(jnp.int32, (tm, gw), 0)
    col = lax.broadcasted_iota(jnp.int32, (tm, gw), 1)
    onehot = jnp.where(col == jnp.where(is_dummy, DEAD_COL, key // SLC_BLOCK), 1.0, 0.0).astype(BF16)
    k_sel = head_norm(sel[:, 0:gw], kg_ref[0:1, :])
    ksa_ref[...] = jnp.concatenate([jnp.where(is_dummy, jnp.zeros_like(k_sel), k_sel), onehot], axis=1)
    kw_ref[...] = head_norm(win[:, 0:gw], kg_ref[1:2, :])

    for j in range(tm // LANES):
        r = slice(j * LANES, (j + 1) * LANES)
        vst_ref[j] = sel[r, gw:].T.astype(BF16)
        vwt_ref[j] = win[r, gw:].T.astype(BF16)

    sub = ATT_TQ // LANES
    zeros = jnp.zeros((dh, hg * ATT_TQ), BF16)
    for j in range(tm // ATT_TQ):
        x_t = [q_ref[(j * sub + u) * LANES:(j * sub + u + 1) * LANES, :].T for u in range(sub)]
        for g in range(NSA_KV_GROUPS):
            heads = []
            for h in range(hg):
                for u in range(sub):
                    seg = x_t[u][(g * hg + h) * dh:(g * hg + h + 1) * dh, :]
                    ms = jnp.mean(seg * seg, axis=0, keepdims=True)
                    heads.append(seg * lax.rsqrt(ms + RMS_EPS) * qg_ref[...] * (dh ** -0.5 * LOG2E))
            blockq = jnp.concatenate(heads, axis=1).astype(BF16)
            qt_ref[j, g] = jnp.concatenate([blockq, zeros] if g == 0 else [zeros, blockq], axis=0)


def _attn_prep(q_a, kv, q_g, k_g, batch, seq_len):
    tm = PROJ_TM
    nt = seq_len // tm
    gw, hg = NSA_KV_WIDTH, NSA_HG
    n_lt = seq_len // LANES
    qg_b = jnp.broadcast_to(q_g[:, None], (NSA_DH, LANES))
    kg_b = jnp.concatenate([k_g[1:3], k_g[1:3]], axis=1)
    last = lambda c: jnp.minimum(c, nt - 1)
    return pl.pallas_call(
        _attn_prep_kernel,
        grid=(batch, nt + 1),
        in_specs=[pl.BlockSpec((tm, NSA_WIDTH), lambda b, c: (b * nt + last(c), 0)),
                  pl.BlockSpec((tm, 2 * gw), lambda b, c: (b * nt + last(c), 1)),
                  pl.BlockSpec((tm, 2 * gw), lambda b, c: (b * nt + last(c), 2)),
                  pl.BlockSpec((NSA_DH, LANES), lambda b, c: (0, 0)),
                  pl.BlockSpec((2, gw), lambda b, c: (0, 0))],
        out_specs=[pl.BlockSpec((None, tm // ATT_TQ, NSA_KV_GROUPS, gw, hg * ATT_TQ),
                                lambda b, c: (b, last(c), 0, 0, 0)),
                   pl.BlockSpec((None, tm, 2 * gw), lambda b, c: (b, c, 0)),
                   pl.BlockSpec((None, tm, gw), lambda b, c: (b, last(c), 0)),
                   pl.BlockSpec((None, tm // LANES, gw, LANES), lambda b, c: (b, last(c), 0, 0)),
                   pl.BlockSpec((None, tm // LANES, gw, LANES), lambda b, c: (b, last(c), 0, 0))],
        out_shape=[jax.ShapeDtypeStruct((batch, seq_len // ATT_TQ, NSA_KV_GROUPS, gw, hg * ATT_TQ), BF16),
                   jax.ShapeDtypeStruct((batch, seq_len + tm, 2 * gw), BF16),
                   jax.ShapeDtypeStruct((batch, seq_len, gw), BF16),
                   jax.ShapeDtypeStruct((batch, n_lt, gw, LANES), BF16),
                   jax.ShapeDtypeStruct((batch, n_lt, gw, LANES), BF16)],
        compiler_params=_params("parallel", "parallel"),
        name="attn_prep",
    )(q_a, kv, kv, qg_b, kg_b)


def _softmax_pv(s, v_t):
    m = jnp.max(s, axis=0, keepdims=True)
    p = jnp.exp2(s - m)
    l = jnp.sum(p, axis=0, keepdims=True)
    return _dot(v_t, p.astype(BF16)) * (1.0 / l)


def _nsa_kernel(qt_ref, kc_ref, vct_ref, ksa_ref, vst_ref, kw_ref, vwt_ref, gate_ref, og_ref,
                o_ref, score_ref, qaug_ref, s0, s1, s_last, p0, p1, a0, a1, m_ref, l_ref, acc_ref,
                *, seq_len):
    tq, hg, dh, gw = ATT_TQ, NSA_HG, NSA_DH, NSA_KV_WIDTH
    rows = hg * tq
    n_cmp = seq_len // CMP_STRIDE
    n_blk = seq_len // SLC_BLOCK
    n_sel = min(SLC_TOPN, n_blk)
    body = SEL_BODY
    i = pl.program_id(1)
    t0 = pl.multiple_of(i * tq, tq)
    qt = qt_ref[...]

    def lane_tile(a):
        return jnp.concatenate([a] * hg, axis=1)

    def v_tiles(vt_ref, k0, tk):
        return jnp.concatenate([vt_ref[k0 // LANES + jt] for jt in range(tk // LANES)], axis=1)

    def fold8(a):
        return a.reshape(a.shape[0] // SUBLANES, SUBLANES, rows)

    span = WINDOW + tq
    w0 = pl.multiple_of(jnp.clip(t0 - WINDOW, 0, seq_len - span), tq)
    k_pos = w0 + lax.broadcasted_iota(jnp.int32, (span, tq), 0)
    q_pos = t0 + lax.broadcasted_iota(jnp.int32, (span, tq), 1)
    w_bias = jnp.where(k_pos <= q_pos, jnp.where(k_pos > q_pos - WINDOW, 0.0, NEG_INF), NEG_INF)
    s_w = _dot(kw_ref[pl.ds(w0, span), :], qt) + lane_tile(w_bias)
    o_win = _softmax_pv(s_w, v_tiles(vwt_ref, w0, span))

    c_end = lax.broadcasted_iota(jnp.int32, (n_cmp, tq), 0) * CMP_STRIDE + (CMP_BLOCK - 1)
    c_ok = c_end <= t0 + lax.broadcasted_iota(jnp.int32, (n_cmp, tq), 1)
    s_c = _dot(kc_ref[...], qt) + lane_tile(jnp.where(c_ok, 0.0, NEG_INF))
    m_c = jnp.max(s_c, axis=0, keepdims=True)
    e_c = jnp.exp2(s_c - m_c) * lane_tile(jnp.where(c_ok, 1.0, 0.0))
    l_c = jnp.sum(e_c, axis=0, keepdims=True)
    p_c = e_c * (1.0 / jnp.where(l_c > 0.0, l_c, 1.0))
    o_cmp = _dot(vct_ref[...], p_c.astype(BF16))

    p_sum = p_c[:, 0:tq]
    for h in range(1, hg):
        p_sum = p_sum + p_c[:, h * tq:(h + 1) * tq]
    blk_r = lax.broadcasted_iota(jnp.int32, (n_blk, n_cmp), 0)
    cmp_c = lax.broadcasted_iota(jnp.int32, (n_blk, n_cmp), 1)
    overlap = ((cmp_c * CMP_STRIDE < (blk_r + 1) * SLC_BLOCK)
               & (cmp_c * CMP_STRIDE + (CMP_BLOCK - 1) >= blk_r * SLC_BLOCK)
               & (cmp_c < n_cmp - 1))
    overlap = jnp.where(overlap, 1.0, 0.0).astype(BF16)
    p_hi, p_lo = _split_hi_lo(p_sum)
    imp = _dot(overlap, p_hi) + _dot(overlap, p_lo)

    blk = lax.broadcasted_iota(jnp.int32, (n_blk, tq), 0)
    t_col = t0 + lax.broadcasted_iota(jnp.int32, (n_blk, tq), 1)
    cur = t_col // SLC_BLOCK
    forced = (blk == 0) | (blk == cur) | (blk == cur - 1)
    score = jnp.where(blk * SLC_BLOCK <= t_col,
                      imp + jnp.where(forced, FORCE_BONUS, 0.0), NEG_INF)
    score_ref[...] = score

    def rank_body(jp, rank):
        row = score_ref[pl.ds(jp, 1), :]
        return rank + jnp.where(blk > jp, jnp.where(row >= score, 1.0, 0.0),
                                jnp.where(row > score, 1.0, 0.0))

    n_live = jnp.minimum((t0 + tq) // SLC_BLOCK, n_blk)
    rank = lax.fori_loop(0, n_live, rank_body, jnp.zeros((n_blk, tq), F32))
    neg = jnp.where(rank < float(n_sel), 0.0, NEG_INF)

    dead = lax.broadcasted_iota(jnp.int32, (gw - n_blk, rows), 0) == DEAD_COL - n_blk
    qaug_ref[0:gw, :] = qt
    qaug_ref[gw:gw + n_blk, :] = lane_tile(neg).astype(BF16)
    qaug_ref[gw + n_blk:2 * gw, :] = jnp.where(dead, NEG_INF, 0.0).astype(BF16)
    q_aug = qaug_ref[...]

    n_full = t0 // body
    k_last = pl.multiple_of(n_full * body, body)

    def scores(n):
        k0 = pl.multiple_of(jnp.where(n < n_full, n * body, seq_len), body)
        return _dot(ksa_ref[pl.ds(k0, body), :], q_aug)

    def pv(n, p_ref, a_ref):
        k0 = pl.multiple_of(jnp.clip(n, 0, seq_len // body - 1) * body, body)
        acc_ref[...] = a_ref[...] * acc_ref[...] + _dot(v_tiles(vst_ref, k0, body), p_ref[...])

    def softmax(s_ref, mx8, p_ref, a_ref):
        m_old = m_ref[...]
        m_new = jnp.maximum(m_old, jnp.max(mx8, axis=0, keepdims=True))
        alpha = jnp.exp2(m_old - m_new)
        p = jnp.exp2(s_ref[...] - m_new)
        l_ref[...] = alpha * l_ref[...] + jnp.sum(fold8(p), axis=0)
        m_ref[...] = m_new
        a_ref[...] = alpha
        p_ref[...] = p.astype(BF16)

    def stage(n, s_cur, mx_cur, p_prev, a_prev, p_cur, a_cur, s_next):
        pv(n - 1, p_prev, a_prev)
        softmax(s_cur, mx_cur, p_cur, a_cur)
        s = scores(n + 1)
        s_next[...] = s
        return jnp.max(fold8(s), axis=0)

    m_ref[...] = jnp.full((1, rows), NEG_INF, F32)
    l_ref[...] = jnp.zeros((SUBLANES, rows), F32)
    acc_ref[...] = jnp.zeros((dh, rows), F32)
    p1[...] = jnp.zeros((body, rows), BF16)
    a1[...] = jnp.ones((1, rows), F32)
    k_b = k_last + lax.broadcasted_iota(jnp.int32, (body, tq), 0)
    q_b = t0 + lax.broadcasted_iota(jnp.int32, (body, tq), 1)
    s = _dot(ksa_ref[pl.ds(k_last, body), :], q_aug) + lane_tile(jnp.where(k_b <= q_b, 0.0, NEG_INF))
    s_last[...] = s
    mx_last = jnp.max(fold8(s), axis=0)
    s = scores(0)
    s0[...] = s

    def sel_body(j, mx0):
        mx1 = stage(2 * j, s0, mx0, p1, a1, p0, a0, s1)
        return stage(2 * j + 1, s1, mx1, p0, a0, p1, a1, s0)

    n_trips = (n_full + 1) // 2
    lax.fori_loop(0, n_trips, sel_body, jnp.max(fold8(s), axis=0))
    pv(2 * n_trips - 1, p1, a1)
    softmax(s_last, mx_last, p0, a0)
    pv(n_full, p0, a0)
    o_sel = acc_ref[...] * (1.0 / jnp.sum(l_ref[...], axis=0, keepdims=True))

    gate_t = _sigmoid(gate_ref[...]).T
    outs = []
    for h in range(hg):
        hs = slice(h * tq, (h + 1) * tq)
        o = (gate_t[3 * h:3 * h + 1] * o_cmp[:, hs] + gate_t[3 * h + 1:3 * h + 2] * o_sel[:, hs]
             + gate_t[3 * h + 2:3 * h + 3] * o_win[:, hs])
        ms = jnp.mean(o * o, axis=0, keepdims=True)
        outs.append(o * lax.rsqrt(ms + RMS_EPS) * og_ref[h * dh:(h + 1) * dh, :])
    o_ref[...] = jnp.concatenate(outs, axis=0).T


def _nsa_attn(q_t, k_cmp, v_cmp_t, ks_aug, vs_t, kw, vw_t, small, out_g_b, batch, seq_len):
    g, hg, dh, tq, gw = NSA_KV_GROUPS, NSA_HG, NSA_DH, ATT_TQ, NSA_KV_WIDTH
    nq = seq_len // tq
    n_cmp = seq_len // CMP_STRIDE
    n_blk = seq_len // SLC_BLOCK
    n_lt = seq_len // LANES
    rows = hg * tq
    assert n_blk <= DEAD_COL and tq % LANES == 0 and SEL_BODY % tq == 0 and seq_len % SEL_BODY == 0
    vt_spec = pl.BlockSpec((None, n_lt, dh, LANES), lambda b, i, j: (b, 0, j, 0))
    return pl.pallas_call(
        functools.partial(_nsa_kernel, seq_len=seq_len),
        grid=(batch, nq, g),
        in_specs=[pl.BlockSpec((None, None, None, gw, rows), lambda b, i, j: (b, i, j, 0, 0)),
                  pl.BlockSpec((None, n_cmp, gw), lambda b, i, j: (b, 0, 0)),
                  pl.BlockSpec((None, dh, n_cmp), lambda b, i, j: (b, j, 0)),
                  pl.BlockSpec((None, seq_len + PROJ_TM, 2 * gw), lambda b, i, j: (b, 0, 0)),
                  vt_spec,
                  pl.BlockSpec((None, seq_len, gw), lambda b, i, j: (b, 0, 0)),
                  vt_spec,
                  pl.BlockSpec((tq, LANES), lambda b, i, j: (b * nq + i, j)),
                  pl.BlockSpec((None, hg * dh, tq), lambda b, i, j: (j, 0, 0))],
        out_specs=pl.BlockSpec((tq, hg * dh), lambda b, i, j: (b * nq + i, j)),
        out_shape=jax.ShapeDtypeStruct((batch * seq_len, NSA_WIDTH), F32),
        scratch_shapes=[pltpu.VMEM((n_blk, tq), F32),
                        pltpu.VMEM((2 * gw, rows), BF16),
                        pltpu.VMEM((SEL_BODY, rows), F32),
                        pltpu.VMEM((SEL_BODY, rows), F32),
                        pltpu.VMEM((SEL_BODY, rows), F32),
                        pltpu.VMEM((SEL_BODY, rows), BF16),
                        pltpu.VMEM((SEL_BODY, rows), BF16),
                        pltpu.VMEM((1, rows), F32),
                        pltpu.VMEM((1, rows), F32),
                        pltpu.VMEM((1, rows), F32),
                        pltpu.VMEM((SUBLANES, rows), F32),
                        pltpu.VMEM((dh, rows), F32)],
        compiler_params=_params("parallel", "arbitrary", "arbitrary"),
        name="nsa_attn",
    )(q_t, k_cmp, v_cmp_t, ks_aug, vs_t, kw, vw_t, small, out_g_b)


def _mlstm_kernel(qk_ref, v_ref, o_ref, ift_ref, cw_ref, cb_ref, fbt_ref, ogt_ref, triu_ref,
                  y_ref, xbuf, ct_st, n_st, m_st):
    lc = M_CHUNK
    nh, dh, width = M_HEADS, M_DH, M_WIDTH
    c = pl.program_id(1)

    @pl.when(c == 0)
    def _():
        xbuf[0:SUBLANES, :] = jnp.zeros((SUBLANES, 2 * width), F32)
        ct_st[...] = jnp.zeros(ct_st.shape, F32)
        n_st[...] = jnp.zeros(n_st.shape, F32)
        m_st[...] = jnp.zeros(m_st.shape, F32)

    u = qk_ref[...]
    xbuf[SUBLANES:SUBLANES + lc, :] = u
    y = cb_ref[...] + cw_ref[0:1, :] * xbuf[pl.ds(SUBLANES - (CONV_WIDTH - 1), lc), :]
    for j in range(1, CONV_WIDTH):
        y = y + cw_ref[j:j + 1, :] * xbuf[pl.ds(SUBLANES - (CONV_WIDTH - 1) + j, lc), :]
    xbuf[0:SUBLANES, :] = u[lc - SUBLANES:lc, :]
    qk = _silu(y)

    ift = ift_ref[...]
    lf_hi, lf_lo = _split_hi_lo(_log_sigmoid(ift + fbt_ref[:, 0:1]))
    b_all = _dot(lf_hi, triu_ref[...]) + _dot(lf_lo, triu_ref[...])
    b_rows = b_all[nh:2 * nh]
    u_rows = ift[0:nh] - b_rows
    b_last = b_rows[:, lc - 1:lc]
    m_prev = m_st[0:nh, 0:1]
    g_rows = u_rows + b_last
    m_new = jnp.maximum(b_last + m_prev, jnp.max(g_rows, axis=1, keepdims=True))
    decay = jnp.exp(b_last + m_prev - m_new)
    ws_rows = jnp.exp(g_rows - m_new)

    def columns(rows):
        return jnp.concatenate([rows, jnp.zeros((LANES - nh, lc), F32)], axis=0).T

    u_cols = columns(u_rows)
    ws_cols = columns(ws_rows)
    ws_b = jnp.concatenate([ws_rows, jnp.zeros((SUBLANES - nh, lc), F32)], axis=0).astype(BF16)

    s_i = lax.broadcasted_iota(jnp.int32, (lc, lc), 0)
    t_i = lax.broadcasted_iota(jnp.int32, (lc, lc), 1)
    causal_t = s_i <= t_i
    v_all = v_ref[...]
    og_all = _sigmoid(o_ref[...])
    for h in range(nh):
        hs = slice(h * dh, (h + 1) * dh)
        q_b = qk[:, h * dh:(h + 1) * dh].astype(BF16)
        k_b = (qk[:, width + h * dh:width + (h + 1) * dh] * (dh ** -0.5)).astype(BF16)
        v_h = v_all[:, hs]
        b_row = b_rows[h:h + 1]
        ct_prev = ct_st[h]
        n_prev = n_st[h:h + 1, :]

        a = b_row + m_prev[h:h + 1]
        d_t = jnp.where(causal_t, u_cols[:, h:h + 1] + b_row, NEG_INF)
        m_t = jnp.maximum(a, jnp.max(d_t, axis=0, keepdims=True))
        w_inter = jnp.exp(a - m_t)
        w_t = jnp.exp(d_t - m_t) * _dot_nt(k_b, q_b)
        num_t = (w_inter * _dot_nt(ct_prev.astype(BF16), q_b)
                 + _dot_tn(v_h.astype(BF16), w_t.astype(BF16)))
        n_b = jnp.broadcast_to(n_prev, (SUBLANES, dh)).astype(BF16)
        den = w_inter * _dot_nt(n_b, q_b)[0:1] + jnp.sum(w_t, axis=0, keepdims=True)
        hid_t = num_t * (1.0 / jnp.maximum(jnp.abs(den), jnp.exp(-m_t)))

        dec = decay[h:h + 1]
        ct_st[h] = dec * ct_prev + _dot_tn((ws_cols[:, h:h + 1] * v_h).astype(BF16), k_b)
        n_st[h:h + 1, :] = dec * n_prev + _dot(ws_b, k_b)[h:h + 1]
        m_st[h:h + 1, :] = jnp.broadcast_to(m_new[h:h + 1], (1, LANES))

        ms = jnp.mean(hid_t * hid_t, axis=0, keepdims=True)
        y_t = hid_t * lax.rsqrt(ms + RMS_EPS) * ogt_ref[h]
        y_ref[:, hs] = og_all[:, hs] * y_t.T


def _mlstm(qk_b, v_b, o_b, if_t, conv_w, conv_b, fb_row, out_g, batch, seq_len):
    n = qk_b.shape[0]
    lc = M_CHUNK
    nc = seq_len // lc
    row = lambda b, c: (b * nc + c, 0)
    og_t = jnp.broadcast_to(out_g.reshape(M_HEADS, M_DH, 1), (M_HEADS, M_DH, lc))
    triu = (jnp.arange(lc)[:, None] <= jnp.arange(lc)[None, :]).astype(BF16)
    return pl.pallas_call(
        _mlstm_kernel,
        grid=(batch, nc),
        in_specs=[pl.BlockSpec((lc, 2 * M_WIDTH), row),
                  pl.BlockSpec((lc, M_WIDTH), row),
                  pl.BlockSpec((lc, M_WIDTH), row),
                  pl.BlockSpec((None, 2 * M_HEADS, lc), lambda b, c: (b, 0, c)),
                  pl.BlockSpec((CONV_WIDTH, 2 * M_WIDTH), lambda b, c: (0, 0)),
                  pl.BlockSpec((1, 2 * M_WIDTH), lambda b, c: (0, 0)),
                  pl.BlockSpec((2 * M_HEADS, LANES), lambda b, c: (0, 0)),
                  pl.BlockSpec((M_HEADS, M_DH, lc), lambda b, c: (0, 0, 0)),
                  pl.BlockSpec((lc, lc), lambda b, c: (0, 0))],
        out_specs=pl.BlockSpec((lc, M_WIDTH), row),
        out_shape=jax.ShapeDtypeStruct((n, M_WIDTH), F32),
        scratch_shapes=[pltpu.VMEM((lc + SUBLANES, 2 * M_WIDTH), F32),
                        pltpu.VMEM((M_HEADS, M_DH, M_DH), F32),
                        pltpu.VMEM((SUBLANES, M_DH), F32),
                        pltpu.VMEM((SUBLANES, LANES), F32)],
        compiler_params=_params("parallel", "arbitrary"),
        name="mlstm",
    )(qk_b, v_b, o_b, if_t, conv_w, conv_b, fb_row, og_t, triu)


def _out_ffn_kernel(x_ref, ya_ref, yb_ref, wo_ref, g_ref, wgu_ref, wd_ref, o_ref):
    x1 = (x_ref[...] + _dot(ya_ref[...].astype(BF16), wo_ref[0:NSA_WIDTH, :])
          + _dot(yb_ref[...].astype(BF16), wo_ref[NSA_WIDTH:, :]))
    h = _rms_rows(x1, g_ref[...]).astype(BF16)
    o_ref[...] = x1
    for c in range(FFN_HIDDEN // FFN_FC):
        cs = slice(c * FFN_FC, (c + 1) * FFN_FC)
        us = slice(FFN_HIDDEN + c * FFN_FC, FFN_HIDDEN + (c + 1) * FFN_FC)
        act = (_silu(_dot(h, wgu_ref[:, cs])) * _dot(h, wgu_ref[:, us])).astype(BF16)
        o_ref[...] += _dot(act, wd_ref[cs, :])


def _out_ffn(x2, y_a, y_b, w_out, ln_g, w_gu, w_down):
    n = x2.shape[0]
    tm = min(PROJ_TM, n)
    row = lambda i: (i, 0)
    return pl.pallas_call(
        _out_ffn_kernel,
        grid=(n // tm,),
        in_specs=[pl.BlockSpec((tm, D_MODEL), row),
                  pl.BlockSpec((tm, NSA_WIDTH), row),
                  pl.BlockSpec((tm, M_WIDTH), row),
                  _resident((D_MODEL, D_MODEL)),
                  _resident((1, D_MODEL)),
                  _resident((D_MODEL, 2 * FFN_HIDDEN)),
                  _resident((FFN_HIDDEN, D_MODEL))],
        out_specs=pl.BlockSpec((tm, D_MODEL), row),
        out_shape=jax.ShapeDtypeStruct((n, D_MODEL), F32),
        compiler_params=_params("parallel"),
        name="out_ffn",
    )(x2, y_a, y_b, w_out, ln_g[None, :], w_gu, w_down)


def _layer(x2, batch, seq_len, ln1_g, w_in, b_in, q_g, k_g, pe, w1, w2, conv_w, conv_b, fgate_b,
           nsa_out_g, m_out_g, w_out, ln2_g, w_gu, w_down):
    g, dh = NSA_KV_GROUPS, NSA_DH
    w_perm, b_perm = _permute_in_proj(w_in, b_in)
    q_a, kv, qk_b, v_b, o_b, small = _in_proj(x2, ln1_g, w_perm, b_perm)

    k_cmp, v_cmp_t = _compress(kv, pe, w1, w2, k_g[0:1], batch, seq_len)
    q_t, ks_aug, kw, vs_t, vw_t = _attn_prep(q_a, kv, q_g, k_g, batch, seq_len)
    out_g_b = jnp.broadcast_to(nsa_out_g.reshape(g, NSA_HG * dh, 1), (g, NSA_HG * dh, ATT_TQ))
    y_a = _nsa_attn(q_t, k_cmp, v_cmp_t, ks_aug, vs_t, kw, vw_t, small, out_g_b, batch, seq_len)

    if_t = small[:, LANES + I_COL:LANES + I_COL + 2 * M_HEADS].reshape(
        batch, seq_len, 2 * M_HEADS).transpose(0, 2, 1)
    fb_row = jnp.zeros((2 * M_HEADS, LANES), F32).at[M_HEADS:, :].set(
        jnp.broadcast_to(fgate_b[:, None], (M_HEADS, LANES)))
    y_b = _mlstm(qk_b, v_b, o_b, if_t, conv_w, conv_b[None, :], fb_row, m_out_g, batch, seq_len)

    return _out_ffn(x2, y_a, y_b, w_out.astype(BF16), ln2_g, w_gu.astype(BF16), w_down.astype(BF16))


def kernel(x, ln1_g, w_in, b_in, nsa_q_norm_g, nsa_k_norm_g, cmp_pe, cmp_w1, cmp_w2, m_conv_w,
           m_conv_b, m_fgate_b, nsa_out_norm_g, m_out_norm_g, w_out, ln2_g, w_gate_up, w_down):
    batch, seq_len, d = x.shape
    x2 = x.reshape(batch * seq_len, d)
    for l in range(ln1_g.shape[0]):
        x2 = _layer(x2, batch, seq_len, ln1_g[l], w_in[l], b_in[l], nsa_q_norm_g[l], nsa_k_norm_g[l],
                    cmp_pe[l], cmp_w1[l], cmp_w2[l], m_conv_w[l], m_conv_b[l], m_fgate_b[l],
                    nsa_out_norm_g[l], m_out_norm_g[l], w_out[l], ln2_g[l], w_gate_up[l], w_down[l])
    return x2.reshape(batch, seq_len, d)
```

```python
import functools

import jax
import jax.numpy as jnp
from jax import lax
from jax.experimental import pallas as pl
from jax.experimental.pallas import tpu as pltpu

F32 = jnp.float32
BF16 = jnp.bfloat16

D_MODEL = 1024
NSA_HEADS = 8
NSA_KV_GROUPS = 2
NSA_HG = NSA_HEADS // NSA_KV_GROUPS
NSA_DH = 64
NSA_WIDTH = NSA_HEADS * NSA_DH
NSA_KV_WIDTH = NSA_KV_GROUPS * NSA_DH
CMP_BLOCK = 32
CMP_STRIDE = 16
CMP_HIDDEN = 2 * NSA_DH
SLC_BLOCK = 64
SLC_TOPN = 16
WINDOW = 512
FORCE_BONUS = 1e4
M_HEADS = 4
M_WIDTH = 512
M_DH = 128
CONV_WIDTH = 4
FFN_HIDDEN = 2816
RMS_EPS = 1e-6
NEG_INF = -1e30
LOG2E = 1.4426950408889634

LANES = 128
SUBLANES = 8
VMEM_LIMIT = 56 * 1024 * 1024

PROJ_TM = 512
ATT_TQ = 256
SEL_BODY = 256
DEAD_COL = 64
M_CHUNK = 256
FFN_FC = 256

SMALL_W = 2 * LANES
GATE_COLS = 3 * NSA_HG
I_COL = GATE_COLS
F_COL = GATE_COLS + M_HEADS


def _dot(a, b):
    return jnp.dot(a, b, preferred_element_type=F32)


def _dot_nt(a, b):
    return lax.dot_general(a, b, (((1,), (1,)), ((), ())), preferred_element_type=F32)


def _dot_tn(a, b):
    return lax.dot_general(a, b, (((0,), (0,)), ((), ())), preferred_element_type=F32)


def _split_hi_lo(a):
    hi = a.astype(BF16)
    lo = (a - hi.astype(F32)).astype(BF16)
    return hi, lo


def _silu(x):
    return x * (1.0 / (1.0 + jnp.exp(-x)))


def _sigmoid(x):
    return 1.0 / (1.0 + jnp.exp(-x))


def _log_sigmoid(x):
    return jnp.minimum(x, 0.0) - jnp.log(1.0 + jnp.exp(-jnp.abs(x)))


def _rms_rows(x, g):
    ms = jnp.mean(x * x, axis=-1, keepdims=True)
    return x * lax.rsqrt(ms + RMS_EPS) * g


def _params(*sem):
    return pltpu.CompilerParams(dimension_semantics=sem, vmem_limit_bytes=VMEM_LIMIT)


def _resident(shape):
    nd = len(shape)
    return pl.BlockSpec(shape, lambda *_: (0,) * nd, pipeline_mode=pl.Buffered(1))


IN_SEGS = (("q_a", NSA_WIDTH), ("kv", 6 * NSA_KV_WIDTH), ("qk_b", 2 * M_WIDTH),
           ("v_b", M_WIDTH), ("o_b", M_WIDTH), ("small", SMALL_W))
IN_COLS = sum(w for _, w in IN_SEGS)


def _in_proj_kernel(x_ref, g_ref, w_ref, b_ref, qg_ref, kg_ref,
                    kcv_ref, qkb_ref, vb_ref, ob_ref, small_ref, qt_ref, ksa_ref, kw_ref, vst_ref, vwt_ref,
                    *, tiles_per_seq):
    x = x_ref[...]
    h = _rms_rows(x, g_ref[...]).astype(BF16)
    seg, c0 = {}, 0
    for name, width in IN_SEGS:
        seg[name] = _dot(h, w_ref[:, c0:c0 + width]) + b_ref[:, c0:c0 + width]
        c0 += width
    qkb_ref[...] = seg["qk_b"]
    vb_ref[...] = seg["v_b"]
    ob_ref[...] = seg["o_b"]
    small_ref[...] = seg["small"]
    kv = seg["kv"]
    kcv_ref[...] = kv[:, 0:2 * NSA_KV_WIDTH]
    t_base = (pl.program_id(0) % tiles_per_seq) * x.shape[0]
    _attn_prep(seg["q_a"], kv[:, 2 * NSA_KV_WIDTH:4 * NSA_KV_WIDTH], kv[:, 4 * NSA_KV_WIDTH:], t_base,
               qg_ref, kg_ref, qt_ref, ksa_ref, kw_ref, vst_ref, vwt_ref)


def _attn_prep(q, sel, win, t_base, qg_ref, kg_ref, qt_ref, ksa_ref, kw_ref, vst_ref, vwt_ref):
    tm = q.shape[0]
    dh, gw, hg = NSA_DH, NSA_KV_WIDTH, NSA_HG
    seg_r = lax.broadcasted_iota(jnp.int32, (gw, gw), 0) // dh
    seg_c = lax.broadcasted_iota(jnp.int32, (gw, gw), 1) // dh
    seg_ones = jnp.where(seg_r == seg_c, 1.0, 0.0).astype(BF16)

    def head_norm(x, g):
        hi, lo = _split_hi_lo(x * x)
        ss = _dot(hi, seg_ones) + _dot(lo, seg_ones)
        return (x * lax.rsqrt(ss * (1.0 / dh) + RMS_EPS) * g).astype(BF16)

    key = t_base + lax.broadcasted_iota(jnp.int32, (tm, gw), 0)
    col = lax.broadcasted_iota(jnp.int32, (tm, gw), 1)
    onehot = jnp.where(col == key // SLC_BLOCK, 1.0, 0.0).astype(BF16)
    ksa_ref[...] = jnp.concatenate([head_norm(sel[:, 0:gw], kg_ref[0:1, :]), onehot], axis=1)
    kw_ref[...] = head_norm(win[:, 0:gw], kg_ref[1:2, :])

    for j in range(tm // LANES):
        r = slice(j * LANES, (j + 1) * LANES)
        vst_ref[j] = sel[r, gw:].T.astype(BF16)
        vwt_ref[j] = win[r, gw:].T.astype(BF16)

    sub = ATT_TQ // LANES
    zeros = jnp.zeros((dh, hg * ATT_TQ), BF16)
    for j in range(tm // ATT_TQ):
        x_t = [q[(j * sub + u) * LANES:(j * sub + u + 1) * LANES, :].T for u in range(sub)]
        for g in range(NSA_KV_GROUPS):
            heads = []
            for h in range(hg):
                for u in range(sub):
                    seg = x_t[u][(g * hg + h) * dh:(g * hg + h + 1) * dh, :]
                    ms = jnp.mean(seg * seg, axis=0, keepdims=True)
                    heads.append(seg * lax.rsqrt(ms + RMS_EPS) * qg_ref[...] * (dh ** -0.5 * LOG2E))
            blockq = jnp.concatenate(heads, axis=1).astype(BF16)
            qt_ref[j, g] = jnp.concatenate([blockq, zeros] if g == 0 else [zeros, blockq], axis=0)


def _permute_in_proj(w_in, b_in):
    o = 0
    parts = {}
    for name, width in (("q_a", NSA_WIDTH), ("kv", 6 * NSA_KV_WIDTH), ("g_a", 3 * NSA_HEADS),
                        ("q_b", M_WIDTH), ("k_b", M_WIDTH), ("v_b", M_WIDTH),
                        ("i_b", M_HEADS), ("f_b", M_HEADS), ("o_b", M_WIDTH)):
        parts[name] = (o, o + width)
        o += width

    def perm(a):
        def sl(name):
            lo, hi = parts[name]
            return a[..., lo:hi]
        lead = a.shape[:-1]
        g_a = sl("g_a")
        small0 = jnp.concatenate([g_a[..., :GATE_COLS],
                                  jnp.zeros(lead + (LANES - GATE_COLS,), a.dtype)], axis=-1)
        small1 = jnp.concatenate([g_a[..., GATE_COLS:], sl("i_b"), sl("f_b"),
                                  jnp.zeros(lead + (LANES - GATE_COLS - 2 * M_HEADS,), a.dtype)], axis=-1)
        return jnp.concatenate([sl("q_a"), sl("kv"), sl("q_b"), sl("k_b"), sl("v_b"), sl("o_b"),
                                small0, small1], axis=-1)

    return perm(w_in).astype(BF16), perm(b_in)[None, :]


def _in_proj(x2, ln_g, w_perm, b_perm, q_g, k_g, batch, seq_len):
    n = x2.shape[0]
    tm = PROJ_TM
    nt = seq_len // tm
    gw, hg, tq = NSA_KV_WIDTH, NSA_HG, ATT_TQ
    assert seq_len % tm == 0 and tm % tq == 0
    qg_b = jnp.broadcast_to(q_g[:, None], (NSA_DH, LANES))
    kg_b = jnp.concatenate([k_g[1:3], k_g[1:3]], axis=1)
    row = lambda i: (i, 0)
    flat = lambda w: (jax.ShapeDtypeStruct((n, w), F32), pl.BlockSpec((tm, w), row))
    seq = lambda shape, blk: (jax.ShapeDtypeStruct((batch,) + shape, BF16),
                              pl.BlockSpec((None,) + blk, lambda i: (i // nt, i % nt) + (0,) * (len(blk) - 1)))
    outs = [flat(2 * gw),
            flat(2 * M_WIDTH), flat(M_WIDTH), flat(M_WIDTH), flat(SMALL_W),
            seq((seq_len // tq, NSA_KV_GROUPS, gw, hg * tq), (tm // tq, NSA_KV_GROUPS, gw, hg * tq)),
            seq((seq_len, 2 * gw), (tm, 2 * gw)),
            seq((seq_len, gw), (tm, gw)),
            seq((seq_len // LANES, gw, LANES), (tm // LANES, gw, LANES)),
            seq((seq_len // LANES, gw, LANES), (tm // LANES, gw, LANES))]
    return pl.pallas_call(
        functools.partial(_in_proj_kernel, tiles_per_seq=nt),
        grid=(n // tm,),
        in_specs=[pl.BlockSpec((tm, D_MODEL), row),
                  _resident((1, D_MODEL)),
                  _resident((D_MODEL, IN_COLS)),
                  _resident((1, IN_COLS)),
                  _resident((NSA_DH, LANES)),
                  _resident((2, gw))],
        out_specs=[o[1] for o in outs],
        out_shape=[o[0] for o in outs],
        compiler_params=_params("parallel"),
        name="in_proj",
    )(x2, ln_g[None, :], w_perm, b_perm, qg_b, kg_b)


def _cmp_kernel(k_ref, v_ref, pe_ref, w1_ref, w2k_ref, w2vt_ref, kg_ref, kc_ref, vct_ref, *, n_cmp):
    acc = [[jnp.zeros((n_cmp, 2 * CMP_HIDDEN), F32) for _ in range(2)] for _ in range(2)]
    for l in range(CMP_STRIDE):
        for w, x_ref in enumerate((k_ref, v_ref)):
            xw = x_ref[pl.ds(l, n_cmp, stride=CMP_STRIDE), :]
            for half in range(2):
                pos = half * CMP_STRIDE + l
                acc[w][half] = acc[w][half] + _dot((xw + pe_ref[w, pos:pos + 1, :]).astype(BF16),
                                                   w1_ref[w, pos])
    act = [_silu(acc[w][0] + pltpu.roll(acc[w][1], n_cmp - 1, 0)).astype(BF16) for w in range(2)]
    k_out = _dot(act[0], w2k_ref[...])
    k_out = jnp.concatenate([_rms_rows(k_out[:, g * NSA_DH:(g + 1) * NSA_DH], kg_ref[...])
                             for g in range(NSA_KV_GROUPS)], axis=1)
    kc_ref[...] = k_out.astype(BF16)
    vct_ref[...] = _dot_nt(w2vt_ref[...], act[1]).astype(BF16)


def _block_diag2(w):
    z = jnp.zeros_like(w)
    return jnp.concatenate([jnp.concatenate([w, z], axis=-1), jnp.concatenate([z, w], axis=-1)], axis=-2)


def _compress(kv, pe, w1, w2, kg, batch, seq_len):
    n_cmp = seq_len // CMP_STRIDE
    dh, gw = NSA_DH, NSA_KV_WIDTH
    pe2 = jnp.concatenate([pe, pe], axis=-1)
    w1b = _block_diag2(w1.reshape(2, CMP_BLOCK, dh, CMP_HIDDEN)).astype(BF16)
    w2b = _block_diag2(w2).astype(BF16)
    w2k, w2v_t = w2b[0], w2b[1].T
    return pl.pallas_call(
        functools.partial(_cmp_kernel, n_cmp=n_cmp),
        grid=(batch,),
        in_specs=[pl.BlockSpec((seq_len, gw), lambda b: (b, 0)),
                  pl.BlockSpec((seq_len, gw), lambda b: (b, 1)),
                  pl.BlockSpec((2, CMP_BLOCK, gw), lambda b: (0, 0, 0)),
                  pl.BlockSpec((2, CMP_BLOCK, gw, 2 * CMP_HIDDEN), lambda b: (0, 0, 0, 0)),
                  pl.BlockSpec((2 * CMP_HIDDEN, gw), lambda b: (0, 0)),
                  pl.BlockSpec((gw, 2 * CMP_HIDDEN), lambda b: (0, 0)),
                  pl.BlockSpec((1, dh), lambda b: (0, 0))],
        out_specs=[pl.BlockSpec((None, n_cmp, gw), lambda b: (b, 0, 0)),
                   pl.BlockSpec((None, gw, n_cmp), lambda b: (b, 0, 0))],
        out_shape=[jax.ShapeDtypeStruct((batch, n_cmp, gw), BF16),
                   jax.ShapeDtypeStruct((batch, gw, n_cmp), BF16)],
        compiler_params=_params("parallel"),
        name="cmp_kv",
    )(kv, kv, pe2, w1b, w2k, w2v_t, kg)


def _softmax_pv(s, v_t):
    m = jnp.max(s, axis=0, keepdims=True)
    p = jnp.exp2(s - m)
    l = jnp.sum(p, axis=0, keepdims=True)
    return _dot(v_t, p.astype(BF16)) * (1.0 / l)


def _nsa_kernel(qt_ref, kc_ref, vct_ref, ksa_ref, dummy_ref, vst_ref, kw_ref, vwt_ref, gate_ref, og_ref,
                o_ref, score_ref, qaug_ref, s0, s1, s_last, p0, p1, a0, a1, m_ref, l_ref, acc_ref,
                *, seq_len):
    tq, hg, dh, gw = ATT_TQ, NSA_HG, NSA_DH, NSA_KV_WIDTH
    rows = hg * tq
    n_cmp = seq_len // CMP_STRIDE
    n_blk = seq_len // SLC_BLOCK
    n_sel = min(SLC_TOPN, n_blk)
    body = SEL_BODY
    i = pl.program_id(1)
    t0 = pl.multiple_of(i * tq, tq)
    qt = qt_ref[...]

    def lane_tile(a):
        return jnp.concatenate([a] * hg, axis=1)

    def v_tiles(vt_ref, k0, tk):
        return jnp.concatenate([vt_ref[k0 // LANES + jt] for jt in range(tk // LANES)], axis=1)

    def fold8(a):
        return a.reshape(a.shape[0] // SUBLANES, SUBLANES, rows)

    span = WINDOW + tq
    w0 = pl.multiple_of(jnp.clip(t0 - WINDOW, 0, seq_len - span), tq)
    below = (lax.broadcasted_iota(jnp.int32, (tq, tq), 0) > lax.broadcasted_iota(jnp.int32, (tq, tq), 1))
    w_bias = []
    for jt in range(span // tq):
        d = t0 - (w0 + jt * tq)
        lower = jnp.where((d >= tq) & (d <= WINDOW), 0.0, NEG_INF)
        upper = jnp.where((d >= 0) & (d < WINDOW), 0.0, NEG_INF)
        w_bias.append(jnp.where(below, lower, upper))
    s_w = _dot(kw_ref[pl.ds(w0, span), :], qt) + lane_tile(jnp.concatenate(w_bias, axis=0))
    o_win = _softmax_pv(s_w, v_tiles(vwt_ref, w0, span))

    c_end = lax.broadcasted_iota(jnp.int32, (n_cmp, tq), 0) * CMP_STRIDE + (CMP_BLOCK - 1)
    c_ok = c_end <= t0 + lax.broadcasted_iota(jnp.int32, (n_cmp, tq), 1)
    s_c = _dot(kc_ref[...], qt) + lane_tile(jnp.where(c_ok, 0.0, NEG_INF))
    m_c = jnp.max(s_c, axis=0, keepdims=True)
    e_c = jnp.exp2(s_c - m_c) * lane_tile(jnp.where(c_ok, 1.0, 0.0))
    l_c = jnp.sum(e_c, axis=0, keepdims=True)
    p_c = e_c * (1.0 / jnp.where(l_c > 0.0, l_c, 1.0))
    o_cmp = _dot(vct_ref[...], p_c.astype(BF16))

    p_sum = p_c[:, 0:tq]
    for h in range(1, hg):
        p_sum = p_sum + p_c[:, h * tq:(h + 1) * tq]
    blk_r = lax.broadcasted_iota(jnp.int32, (n_blk, n_cmp), 0)
    cmp_c = lax.broadcasted_iota(jnp.int32, (n_blk, n_cmp), 1)
    overlap = ((cmp_c * CMP_STRIDE < (blk_r + 1) * SLC_BLOCK)
               & (cmp_c * CMP_STRIDE + (CMP_BLOCK - 1) >= blk_r * SLC_BLOCK)
               & (cmp_c < n_cmp - 1))
    overlap = jnp.where(overlap, 1.0, 0.0).astype(BF16)
    p_hi, p_lo = _split_hi_lo(p_sum)
    imp = _dot(overlap, p_hi) + _dot(overlap, p_lo)

    blk = lax.broadcasted_iota(jnp.int32, (n_blk, tq), 0)
    t_col = t0 + lax.broadcasted_iota(jnp.int32, (n_blk, tq), 1)
    cur = t_col // SLC_BLOCK
    forced = (blk == 0) | (blk == cur) | (blk == cur - 1)
    score = jnp.where(blk * SLC_BLOCK <= t_col,
                      imp + jnp.where(forced, FORCE_BONUS, 0.0), NEG_INF)
    score_ref[...] = score

    def rank_body(jp, rank):
        row = score_ref[pl.ds(jp, 1), :]
        return rank + jnp.where(blk > jp, jnp.where(row >= score, 1.0, 0.0),
                                jnp.where(row > score, 1.0, 0.0))

    n_live = jnp.minimum((t0 + tq) // SLC_BLOCK, n_blk)
    rank = lax.fori_loop(0, n_live, rank_body, jnp.zeros((n_blk, tq), F32))
    neg = jnp.where(rank < float(n_sel), 0.0, NEG_INF)

    dead = lax.broadcasted_iota(jnp.int32, (gw - n_blk, rows), 0) == DEAD_COL - n_blk
    qaug_ref[0:gw, :] = qt
    qaug_ref[gw:gw + n_blk, :] = lane_tile(neg).astype(BF16)
    qaug_ref[gw + n_blk:2 * gw, :] = jnp.where(dead, NEG_INF, 0.0).astype(BF16)
    q_aug = qaug_ref[...]

    n_full = t0 // body
    k_last = pl.multiple_of(n_full * body, body)

    def scores(n):
        k0 = pl.multiple_of(jnp.minimum(n, seq_len // body - 1) * body, body)
        return _dot(jnp.where(n < n_full, ksa_ref[pl.ds(k0, body), :], dummy_ref[...]), q_aug)

    def pv(n, p_ref, a_ref):
        k0 = pl.multiple_of(jnp.clip(n, 0, seq_len // body - 1) * body, body)
        acc_ref[...] = a_ref[...] * acc_ref[...] + _dot(v_tiles(vst_ref, k0, body), p_ref[...])

    def softmax(s_ref, mx8, p_ref, a_ref):
        m_old = m_ref[...]
        m_new = jnp.maximum(m_old, jnp.max(mx8, axis=0, keepdims=True))
        alpha = jnp.exp2(m_old - m_new)
        p = jnp.exp2(s_ref[...] - m_new)
        l_ref[...] = alpha * l_ref[...] + jnp.sum(fold8(p), axis=0)
        m_ref[...] = m_new
        a_ref[...] = alpha
        p_ref[...] = p.astype(BF16)

    def stage(n, s_cur, mx_cur, p_prev, a_prev, p_cur, a_cur, s_next):
        pv(n - 1, p_prev, a_prev)
        softmax(s_cur, mx_cur, p_cur, a_cur)
        s = scores(n + 1)
        s_next[...] = s
        return jnp.max(fold8(s), axis=0)

    m_ref[...] = jnp.full((1, rows), NEG_INF, F32)
    l_ref[...] = jnp.zeros((SUBLANES, rows), F32)
    acc_ref[...] = jnp.zeros((dh, rows), F32)
    p1[...] = jnp.zeros((body, rows), BF16)
    a1[...] = jnp.ones((1, rows), F32)
    k_b = k_last + lax.broadcasted_iota(jnp.int32, (body, tq), 0)
    q_b = t0 + lax.broadcasted_iota(jnp.int32, (body, tq), 1)
    s = _dot(ksa_ref[pl.ds(k_last, body), :], q_aug) + lane_tile(jnp.where(k_b <= q_b, 0.0, NEG_INF))
    s_last[...] = s
    mx_last = jnp.max(fold8(s), axis=0)
    s = scores(0)
    s0[...] = s

    def sel_body(j, mx0):
        mx1 = stage(2 * j, s0, mx0, p1, a1, p0, a0, s1)
        return stage(2 * j + 1, s1, mx1, p0, a0, p1, a1, s0)

    n_trips = (n_full + 1) // 2
    lax.fori_loop(0, n_trips, sel_body, jnp.max(fold8(s), axis=0))
    pv(2 * n_trips - 1, p1, a1)
    softmax(s_last, mx_last, p0, a0)
    pv(n_full, p0, a0)
    o_sel = acc_ref[...] * (1.0 / jnp.sum(l_ref[...], axis=0, keepdims=True))

    gate_t = _sigmoid(gate_ref[...]).T
    outs = []
    for h in range(hg):
        hs = slice(h * tq, (h + 1) * tq)
        o = (gate_t[3 * h:3 * h + 1] * o_cmp[:, hs] + gate_t[3 * h + 1:3 * h + 2] * o_sel[:, hs]
             + gate_t[3 * h + 2:3 * h + 3] * o_win[:, hs])
        ms = jnp.mean(o * o, axis=0, keepdims=True)
        outs.append(o * lax.rsqrt(ms + RMS_EPS) * og_ref[h * dh:(h + 1) * dh, :])
    o_ref[...] = jnp.concatenate(outs, axis=0).T


def _nsa_attn(q_t, k_cmp, v_cmp_t, ks_aug, vs_t, kw, vw_t, small, out_g_b, batch, seq_len):
    dummy = jnp.zeros((SEL_BODY, 2 * NSA_KV_WIDTH), BF16).at[:, NSA_KV_WIDTH + DEAD_COL].set(1.0)
    g, hg, dh, tq, gw = NSA_KV_GROUPS, NSA_HG, NSA_DH, ATT_TQ, NSA_KV_WIDTH
    nq = seq_len // tq
    n_cmp = seq_len // CMP_STRIDE
    n_blk = seq_len // SLC_BLOCK
    n_lt = seq_len // LANES
    rows = hg * tq
    assert n_blk <= DEAD_COL and tq % LANES == 0 and SEL_BODY % tq == 0 and seq_len % SEL_BODY == 0
    assert WINDOW % tq == 0 and seq_len >= WINDOW + tq
    vt_spec = pl.BlockSpec((None, n_lt, dh, LANES), lambda b, i, j: (b, 0, j, 0))
    return pl.pallas_call(
        functools.partial(_nsa_kernel, seq_len=seq_len),
        grid=(batch, nq, g),
        in_specs=[pl.BlockSpec((None, None, None, gw, rows), lambda b, i, j: (b, i, j, 0, 0)),
                  pl.BlockSpec((None, n_cmp, gw), lambda b, i, j: (b, 0, 0)),
                  pl.BlockSpec((None, dh, n_cmp), lambda b, i, j: (b, j, 0)),
                  pl.BlockSpec((None, seq_len, 2 * gw), lambda b, i, j: (b, 0, 0)),
                  pl.BlockSpec((SEL_BODY, 2 * gw), lambda b, i, j: (0, 0)),
                  vt_spec,
                  pl.BlockSpec((None, seq_len, gw), lambda b, i, j: (b, 0, 0)),
                  vt_spec,
                  pl.BlockSpec((tq, LANES), lambda b, i, j: (b * nq + i, j)),
                  pl.BlockSpec((None, hg * dh, tq), lambda b, i, j: (j, 0, 0))],
        out_specs=pl.BlockSpec((tq, hg * dh), lambda b, i, j: (b * nq + i, j)),
        out_shape=jax.ShapeDtypeStruct((batch * seq_len, NSA_WIDTH), F32),
        scratch_shapes=[pltpu.VMEM((n_blk, tq), F32),
                        pltpu.VMEM((2 * gw, rows), BF16),
                        pltpu.VMEM((SEL_BODY, rows), F32),
                        pltpu.VMEM((SEL_BODY, rows), F32),
                        pltpu.VMEM((SEL_BODY, rows), F32),
                        pltpu.VMEM((SEL_BODY, rows), BF16),
                        pltpu.VMEM((SEL_BODY, rows), BF16),
                        pltpu.VMEM((1, rows), F32),
                        pltpu.VMEM((1, rows), F32),
                        pltpu.VMEM((1, rows), F32),
                        pltpu.VMEM((SUBLANES, rows), F32),
                        pltpu.VMEM((dh, rows), F32)],
        compiler_params=_params("parallel", "arbitrary", "arbitrary"),
        name="nsa_attn",
    )(q_t, k_cmp, v_cmp_t, ks_aug, dummy, vs_t, kw, vw_t, small, out_g_b)


def _mlstm_kernel(qk_ref, v_ref, o_ref, ift_ref, cw_ref, cb_ref, fbt_ref, ogt_ref, triu_ref,
                  y_ref, xbuf, ct_st, n_st, m_st):
    lc = M_CHUNK
    nh, dh, width = M_HEADS, M_DH, M_WIDTH
    c = pl.program_id(1)

    @pl.when(c == 0)
    def _():
        xbuf[0:SUBLANES, :] = jnp.zeros((SUBLANES, 2 * width), F32)
        ct_st[...] = jnp.zeros(ct_st.shape, F32)
        n_st[...] = jnp.zeros(n_st.shape, F32)
        m_st[...] = jnp.zeros(m_st.shape, F32)

    u = qk_ref[...]
    xbuf[SUBLANES:SUBLANES + lc, :] = u
    y = cb_ref[...] + cw_ref[0:1, :] * xbuf[pl.ds(SUBLANES - (CONV_WIDTH - 1), lc), :]
    for j in range(1, CONV_WIDTH):
        y = y + cw_ref[j:j + 1, :] * xbuf[pl.ds(SUBLANES - (CONV_WIDTH - 1) + j, lc), :]
    xbuf[0:SUBLANES, :] = u[lc - SUBLANES:lc, :]
    qk = _silu(y)

    ift = ift_ref[...]
    lf_hi, lf_lo = _split_hi_lo(_log_sigmoid(ift + fbt_ref[:, 0:1]))
    b_all = _dot(lf_hi, triu_ref[...]) + _dot(lf_lo, triu_ref[...])
    b_rows = b_all[nh:2 * nh]
    u_rows = ift[0:nh] - b_rows
    b_last = b_rows[:, lc - 1:lc]
    m_prev = m_st[0:nh, 0:1]
    g_rows = u_rows + b_last
    m_new = jnp.maximum(b_last + m_prev, jnp.max(g_rows, axis=1, keepdims=True))
    decay = jnp.exp(b_last + m_prev - m_new)
    ws_rows = jnp.exp(g_rows - m_new)

    def columns(rows):
        return jnp.concatenate([rows, jnp.zeros((LANES - nh, lc), F32)], axis=0).T

    u_cols = columns(u_rows)
    ws_cols = columns(ws_rows)
    ws_b = jnp.concatenate([ws_rows, jnp.zeros((SUBLANES - nh, lc), F32)], axis=0).astype(BF16)

    s_i = lax.broadcasted_iota(jnp.int32, (lc, lc), 0)
    t_i = lax.broadcasted_iota(jnp.int32, (lc, lc), 1)
    causal_t = s_i <= t_i
    v_all = v_ref[...]
    og_all = _sigmoid(o_ref[...])
    for h in range(nh):
        hs = slice(h * dh, (h + 1) * dh)
        q_b = qk[:, h * dh:(h + 1) * dh].astype(BF16)
        k_b = (qk[:, width + h * dh:width + (h + 1) * dh] * (dh ** -0.5)).astype(BF16)
        v_h = v_all[:, hs]
        b_row = b_rows[h:h + 1]
        ct_prev = ct_st[h]
        n_prev = n_st[h:h + 1, :]

        a = b_row + m_prev[h:h + 1]
        d_t = jnp.where(causal_t, u_cols[:, h:h + 1] + b_row, NEG_INF)
        m_t = jnp.maximum(a, jnp.max(d_t, axis=0, keepdims=True))
        w_inter = jnp.exp(a - m_t)
        w_t = jnp.exp(d_t - m_t) * _dot_nt(k_b, q_b)
        num_t = (w_inter * _dot_nt(ct_prev.astype(BF16), q_b)
                 + _dot_tn(v_h.astype(BF16), w_t.astype(BF16)))
        n_b = jnp.broadcast_to(n_prev, (SUBLANES, dh)).astype(BF16)
        den = w_inter * _dot_nt(n_b, q_b)[0:1] + jnp.sum(w_t, axis=0, keepdims=True)
        hid_t = num_t * (1.0 / jnp.maximum(jnp.abs(den), jnp.exp(-m_t)))

        dec = decay[h:h + 1]
        ct_st[h] = dec * ct_prev + _dot_tn((ws_cols[:, h:h + 1] * v_h).astype(BF16), k_b)
        n_st[h:h + 1, :] = dec * n_prev + _dot(ws_b, k_b)[h:h + 1]
        m_st[h:h + 1, :] = jnp.broadcast_to(m_new[h:h + 1], (1, LANES))

        ms = jnp.mean(hid_t * hid_t, axis=0, keepdims=True)
        y_t = hid_t * lax.rsqrt(ms + RMS_EPS) * ogt_ref[h]
        y_ref[:, hs] = og_all[:, hs] * y_t.T


def _mlstm(qk_b, v_b, o_b, if_t, conv_w, conv_b, fb_row, out_g, batch, seq_len):
    n = qk_b.shape[0]
    lc = M_CHUNK
    nc = seq_len // lc
    row = lambda b, c: (b * nc + c, 0)
    og_t = jnp.broadcast_to(out_g.reshape(M_HEADS, M_DH, 1), (M_HEADS, M_DH, lc))
    triu = (jnp.arange(lc)[:, None] <= jnp.arange(lc)[None, :]).astype(BF16)
    return pl.pallas_call(
        _mlstm_kernel,
        grid=(batch, nc),
        in_specs=[pl.BlockSpec((lc, 2 * M_WIDTH), row),
                  pl.BlockSpec((lc, M_WIDTH), row),
                  pl.BlockSpec((lc, M_WIDTH), row),
                  pl.BlockSpec((None, 2 * M_HEADS, lc), lambda b, c: (b, 0, c)),
                  pl.BlockSpec((CONV_WIDTH, 2 * M_WIDTH), lambda b, c: (0, 0)),
                  pl.BlockSpec((1, 2 * M_WIDTH), lambda b, c: (0, 0)),
                  pl.BlockSpec((2 * M_HEADS, LANES), lambda b, c: (0, 0)),
                  pl.BlockSpec((M_HEADS, M_DH, lc), lambda b, c: (0, 0, 0)),
                  pl.BlockSpec((lc, lc), lambda b, c: (0, 0))],
        out_specs=pl.BlockSpec((lc, M_WIDTH), row),
        out_shape=jax.ShapeDtypeStruct((n, M_WIDTH), F32),
        scratch_shapes=[pltpu.VMEM((lc + SUBLANES, 2 * M_WIDTH), F32),
                        pltpu.VMEM((M_HEADS, M_DH, M_DH), F32),
                        pltpu.VMEM((SUBLANES, M_DH), F32),
                        pltpu.VMEM((SUBLANES, LANES), F32)],
        compiler_params=_params("parallel", "arbitrary"),
        name="mlstm",
    )(qk_b, v_b, o_b, if_t, conv_w, conv_b, fb_row, og_t, triu)


def _out_ffn_kernel(x_ref, ya_ref, yb_ref, wo_ref, g_ref, wgu_ref, wd_ref, o_ref):
    x1 = (x_ref[...] + _dot(ya_ref[...].astype(BF16), wo_ref[0:NSA_WIDTH, :])
          + _dot(yb_ref[...].astype(BF16), wo_ref[NSA_WIDTH:, :]))
    h = _rms_rows(x1, g_ref[...]).astype(BF16)
    o_ref[...] = x1
    for c in range(FFN_HIDDEN // FFN_FC):
        cs = slice(c * FFN_FC, (c + 1) * FFN_FC)
        us = slice(FFN_HIDDEN + c * FFN_FC, FFN_HIDDEN + (c + 1) * FFN_FC)
        act = (_silu(_dot(h, wgu_ref[:, cs])) * _dot(h, wgu_ref[:, us])).astype(BF16)
        o_ref[...] += _dot(act, wd_ref[cs, :])


def _out_ffn(x2, y_a, y_b, w_out, ln_g, w_gu, w_down):
    n = x2.shape[0]
    tm = min(PROJ_TM, n)
    row = lambda i: (i, 0)
    return pl.pallas_call(
        _out_ffn_kernel,
        grid=(n // tm,),
        in_specs=[pl.BlockSpec((tm, D_MODEL), row),
                  pl.BlockSpec((tm, NSA_WIDTH), row),
                  pl.BlockSpec((tm, M_WIDTH), row),
                  _resident((D_MODEL, D_MODEL)),
                  _resident((1, D_MODEL)),
                  _resident((D_MODEL, 2 * FFN_HIDDEN)),
                  _resident((FFN_HIDDEN, D_MODEL))],
        out_specs=pl.BlockSpec((tm, D_MODEL), row),
        out_shape=jax.ShapeDtypeStruct((n, D_MODEL), F32),
        compiler_params=_params("parallel"),
        name="out_ffn",
    )(x2, y_a, y_b, w_out, ln_g[None, :], w_gu, w_down)


def _layer(x2, batch, seq_len, ln1_g, w_in, b_in, q_g, k_g, pe, w1, w2, conv_w, conv_b, fgate_b,
           nsa_out_g, m_out_g, w_out, ln2_g, w_gu, w_down):
    g, dh = NSA_KV_GROUPS, NSA_DH
    w_perm, b_perm = _permute_in_proj(w_in, b_in)
    kcv, qk_b, v_b, o_b, small, q_t, ks_aug, kw, vs_t, vw_t = _in_proj(
        x2, ln1_g, w_perm, b_perm, q_g, k_g, batch, seq_len)
    k_cmp, v_cmp_t = _compress(kcv, pe, w1, w2, k_g[0:1], batch, seq_len)
    out_g_b = jnp.broadcast_to(nsa_out_g.reshape(g, NSA_HG * dh, 1), (g, NSA_HG * dh, ATT_TQ))
    y_a = _nsa_attn(q_t, k_cmp, v_cmp_t, ks_aug, vs_t, kw, vw_t, small, out_g_b, batch, seq_len)

    if_t = small[:, LANES + I_COL:LANES + I_COL + 2 * M_HEADS].reshape(
        batch, seq_len, 2 * M_HEADS).transpose(0, 2, 1)
    fb_row = jnp.zeros((2 * M_HEADS, LANES), F32).at[M_HEADS:, :].set(
        jnp.broadcast_to(fgate_b[:, None], (M_HEADS, LANES)))
    y_b = _mlstm(qk_b, v_b, o_b, if_t, conv_w, conv_b[None, :], fb_row, m_out_g, batch, seq_len)

    return _out_ffn(x2, y_a, y_b, w_out.astype(BF16), ln2_g, w_gu.astype(BF16), w_down.astype(BF16))


def kernel(x, ln1_g, w_in, b_in, nsa_q_norm_g, nsa_k_norm_g, cmp_pe, cmp_w1, cmp_w2, m_conv_w,
           m_conv_b, m_fgate_b, nsa_out_norm_g, m_out_norm_g, w_out, ln2_g, w_gate_up, w_down):
    batch, seq_len, d = x.shape
    x2 = x.reshape(batch * seq_len, d)
    for l in range(ln1_g.shape[0]):
        x2 = _layer(x2, batch, seq_len, ln1_g[l], w_in[l], b_in[l], nsa_q_norm_g[l], nsa_k_norm_g[l],
                    cmp_pe[l], cmp_w1[l], cmp_w2[l], m_conv_w[l], m_conv_b[l], m_fgate_b[l],
                    nsa_out_norm_g[l], m_out_norm_g[l], w_out[l], ln2_g[l], w_gate_up[l], w_down[l])
    return x2.reshape(batch, seq_len, d)
```

```python
import functools

import jax
import jax.numpy as jnp
from jax import lax
from jax.experimental import pallas as pl
from jax.experimental.pallas import tpu as pltpu

F32 = jnp.float32
BF16 = jnp.bfloat16

D_MODEL = 1024
NSA_HEADS = 8
NSA_KV_GROUPS = 2
NSA_HG = NSA_HEADS // NSA_KV_GROUPS
NSA_DH = 64
NSA_WIDTH = NSA_HEADS * NSA_DH
NSA_KV_WIDTH = NSA_KV_GROUPS * NSA_DH
CMP_BLOCK = 32
CMP_STRIDE = 16
CMP_HIDDEN = 2 * NSA_DH
SLC_BLOCK = 64
SLC_TOPN = 16
WINDOW = 512
FORCE_BONUS = 1e4
M_HEADS = 4
M_WIDTH = 512
M_DH = 128
CONV_WIDTH = 4
FFN_HIDDEN = 2816
RMS_EPS = 1e-6
NEG_INF = -1e30
LOG2E = 1.4426950408889634

LANES = 128
SUBLANES = 8
VMEM_LIMIT = 56 * 1024 * 1024

PROJ_TM = 512
ATT_TQ = 256
SEL_BODY = 256
DEAD_COL = 64
M_CHUNK = 256
FFN_FC = 256

SMALL_W = 2 * LANES
GATE_COLS = 3 * NSA_HG
I_COL = GATE_COLS
F_COL = GATE_COLS + M_HEADS


def _dot(a, b):
    return jnp.dot(a, b, preferred_element_type=F32)


def _dot_nt(a, b):
    return lax.dot_general(a, b, (((1,), (1,)), ((), ())), preferred_element_type=F32)


def _dot_tn(a, b):
    return lax.dot_general(a, b, (((0,), (0,)), ((), ())), preferred_element_type=F32)


def _split_hi_lo(a):
    hi = a.astype(BF16)
    lo = (a - hi.astype(F32)).astype(BF16)
    return hi, lo


def _silu(x):
    return x * (1.0 / (1.0 + jnp.exp(-x)))


def _sigmoid(x):
    return 1.0 / (1.0 + jnp.exp(-x))


def _log_sigmoid(x):
    return jnp.minimum(x, 0.0) - jnp.log(1.0 + jnp.exp(-jnp.abs(x)))


def _rms_rows(x, g):
    ms = jnp.mean(x * x, axis=-1, keepdims=True)
    return x * lax.rsqrt(ms + RMS_EPS) * g


def _params(*sem):
    return pltpu.CompilerParams(dimension_semantics=sem, vmem_limit_bytes=VMEM_LIMIT)


def _resident(shape):
    nd = len(shape)
    return pl.BlockSpec(shape, lambda *_: (0,) * nd, pipeline_mode=pl.Buffered(1))


IN_SEGS = (("q_a", NSA_WIDTH), ("kv", 6 * NSA_KV_WIDTH), ("qk_b", 2 * M_WIDTH),
           ("v_b", M_WIDTH), ("o_b", M_WIDTH), ("small", SMALL_W))
IN_COLS = sum(w for _, w in IN_SEGS)


def _in_proj_kernel(x_ref, g_ref, w_ref, b_ref, qg_ref, kg_ref,
                    kcv_ref, qkb_ref, vb_ref, ob_ref, small_ref, qt_ref, ksa_ref, kw_ref, vst_ref, vwt_ref,
                    *, tiles_per_seq):
    x = x_ref[...]
    h = _rms_rows(x, g_ref[...]).astype(BF16)
    seg, c0 = {}, 0
    for name, width in IN_SEGS:
        seg[name] = _dot(h, w_ref[:, c0:c0 + width]) + b_ref[:, c0:c0 + width]
        c0 += width
    qkb_ref[...] = seg["qk_b"]
    vb_ref[...] = seg["v_b"]
    ob_ref[...] = seg["o_b"]
    small_ref[...] = seg["small"]
    kv = seg["kv"]
    kcv_ref[...] = kv[:, 0:2 * NSA_KV_WIDTH]
    t_base = (pl.program_id(0) % tiles_per_seq) * x.shape[0]
    _attn_prep(seg["q_a"], kv[:, 2 * NSA_KV_WIDTH:4 * NSA_KV_WIDTH], kv[:, 4 * NSA_KV_WIDTH:], t_base,
               qg_ref, kg_ref, qt_ref, ksa_ref, kw_ref, vst_ref, vwt_ref)


def _attn_prep(q, sel, win, t_base, qg_ref, kg_ref, qt_ref, ksa_ref, kw_ref, vst_ref, vwt_ref):
    tm = q.shape[0]
    dh, gw, hg = NSA_DH, NSA_KV_WIDTH, NSA_HG
    seg_r = lax.broadcasted_iota(jnp.int32, (gw, gw), 0) // dh
    seg_c = lax.broadcasted_iota(jnp.int32, (gw, gw), 1) // dh
    seg_ones = jnp.where(seg_r == seg_c, 1.0, 0.0).astype(BF16)

    def head_norm(x, g):
        hi, lo = _split_hi_lo(x * x)
        ss = _dot(hi, seg_ones) + _dot(lo, seg_ones)
        return (x * lax.rsqrt(ss * (1.0 / dh) + RMS_EPS) * g).astype(BF16)

    key = t_base + lax.broadcasted_iota(jnp.int32, (tm, gw), 0)
    col = lax.broadcasted_iota(jnp.int32, (tm, gw), 1)
    onehot = jnp.where(col == key // SLC_BLOCK, 1.0, 0.0).astype(BF16)
    ksa_ref[...] = jnp.concatenate([head_norm(sel[:, 0:gw], kg_ref[0:1, :]), onehot], axis=1)
    kw_ref[...] = head_norm(win[:, 0:gw], kg_ref[1:2, :])

    for j in range(tm // LANES):
        r = slice(j * LANES, (j + 1) * LANES)
        vst_ref[j] = sel[r, gw:].T.astype(BF16)
        vwt_ref[j] = win[r, gw:].T.astype(BF16)

    sub = ATT_TQ // LANES
    zeros = jnp.zeros((dh, hg * ATT_TQ), BF16)
    for j in range(tm // ATT_TQ):
        x_t = [q[(j * sub + u) * LANES:(j * sub + u + 1) * LANES, :].T for u in range(sub)]
        for g in range(NSA_KV_GROUPS):
            heads = []
            for h in range(hg):
                for u in range(sub):
                    seg = x_t[u][(g * hg + h) * dh:(g * hg + h + 1) * dh, :]
                    ms = jnp.mean(seg * seg, axis=0, keepdims=True)
                    heads.append(seg * lax.rsqrt(ms + RMS_EPS) * qg_ref[...] * (dh ** -0.5 * LOG2E))
            blockq = jnp.concatenate(heads, axis=1).astype(BF16)
            qt_ref[j, g] = jnp.concatenate([blockq, zeros] if g == 0 else [zeros, blockq], axis=0)


def _permute_in_proj(w_in, b_in):
    o = 0
    parts = {}
    for name, width in (("q_a", NSA_WIDTH), ("kv", 6 * NSA_KV_WIDTH), ("g_a", 3 * NSA_HEADS),
                        ("q_b", M_WIDTH), ("k_b", M_WIDTH), ("v_b", M_WIDTH),
                        ("i_b", M_HEADS), ("f_b", M_HEADS), ("o_b", M_WIDTH)):
        parts[name] = (o, o + width)
        o += width

    def perm(a):
        def sl(name):
            lo, hi = parts[name]
            return a[..., lo:hi]
        lead = a.shape[:-1]
        g_a = sl("g_a")
        small0 = jnp.concatenate([g_a[..., :GATE_COLS],
                                  jnp.zeros(lead + (LANES - GATE_COLS,), a.dtype)], axis=-1)
        small1 = jnp.concatenate([g_a[..., GATE_COLS:], sl("i_b"), sl("f_b"),
                                  jnp.zeros(lead + (LANES - GATE_COLS - 2 * M_HEADS,), a.dtype)], axis=-1)
        return jnp.concatenate([sl("q_a"), sl("kv"), sl("q_b"), sl("k_b"), sl("v_b"), sl("o_b"),
                                small0, small1], axis=-1)

    return perm(w_in).astype(BF16), perm(b_in)[None, :]


def _in_proj(x2, ln_g, w_perm, b_perm, q_g, k_g, batch, seq_len):
    n = x2.shape[0]
    tm = PROJ_TM
    nt = seq_len // tm
    gw, hg, tq = NSA_KV_WIDTH, NSA_HG, ATT_TQ
    assert seq_len % tm == 0 and tm % tq == 0
    qg_b = jnp.broadcast_to(q_g[:, None], (NSA_DH, LANES))
    kg_b = jnp.concatenate([k_g[1:3], k_g[1:3]], axis=1)
    row = lambda i: (i, 0)
    flat = lambda w: (jax.ShapeDtypeStruct((n, w), F32), pl.BlockSpec((tm, w), row))
    seq = lambda shape, blk: (jax.ShapeDtypeStruct((batch,) + shape, BF16),
                              pl.BlockSpec((None,) + blk, lambda i: (i // nt, i % nt) + (0,) * (len(blk) - 1)))
    outs = [flat(2 * gw),
            flat(2 * M_WIDTH), flat(M_WIDTH), flat(M_WIDTH), flat(SMALL_W),
            seq((seq_len // tq, NSA_KV_GROUPS, gw, hg * tq), (tm // tq, NSA_KV_GROUPS, gw, hg * tq)),
            seq((seq_len, 2 * gw), (tm, 2 * gw)),
            seq((seq_len, gw), (tm, gw)),
            seq((seq_len // LANES, gw, LANES), (tm // LANES, gw, LANES)),
            seq((seq_len // LANES, gw, LANES), (tm // LANES, gw, LANES))]
    return pl.pallas_call(
        functools.partial(_in_proj_kernel, tiles_per_seq=nt),
        grid=(n // tm,),
        in_specs=[pl.BlockSpec((tm, D_MODEL), row),
                  _resident((1, D_MODEL)),
                  _resident((D_MODEL, IN_COLS)),
                  _resident((1, IN_COLS)),
                  _resident((NSA_DH, LANES)),
                  _resident((2, gw))],
        out_specs=[o[1] for o in outs],
        out_shape=[o[0] for o in outs],
        compiler_params=_params("parallel"),
        name="in_proj",
    )(x2, ln_g[None, :], w_perm, b_perm, qg_b, kg_b)


def _cmp_kernel(k_ref, v_ref, pe_ref, w1_ref, w2k_ref, w2vt_ref, kg_ref, kc_ref, vct_ref, *, n_cmp):
    acc = [[jnp.zeros((n_cmp, 2 * CMP_HIDDEN), F32) for _ in range(2)] for _ in range(2)]
    for l in range(CMP_STRIDE):
        for w, x_ref in enumerate((k_ref, v_ref)):
            xw = x_ref[pl.ds(l, n_cmp, stride=CMP_STRIDE), :]
            for half in range(2):
                pos = half * CMP_STRIDE + l
                acc[w][half] = acc[w][half] + _dot((xw + pe_ref[w, pos:pos + 1, :]).astype(BF16),
                                                   w1_ref[w, pos])
    act = [_silu(acc[w][0] + pltpu.roll(acc[w][1], n_cmp - 1, 0)).astype(BF16) for w in range(2)]
    k_out = _dot(act[0], w2k_ref[...])
    k_out = jnp.concatenate([_rms_rows(k_out[:, g * NSA_DH:(g + 1) * NSA_DH], kg_ref[...])
                             for g in range(NSA_KV_GROUPS)], axis=1)
    kc_ref[...] = k_out.astype(BF16)
    vct_ref[...] = _dot_nt(w2vt_ref[...], act[1]).astype(BF16)


def _block_diag2(w):
    z = jnp.zeros_like(w)
    return jnp.concatenate([jnp.concatenate([w, z], axis=-1), jnp.concatenate([z, w], axis=-1)], axis=-2)


def _compress(kv, pe, w1, w2, kg, batch, seq_len):
    n_cmp = seq_len // CMP_STRIDE
    dh, gw = NSA_DH, NSA_KV_WIDTH
    pe2 = jnp.concatenate([pe, pe], axis=-1)
    w1b = _block_diag2(w1.reshape(2, CMP_BLOCK, dh, CMP_HIDDEN)).astype(BF16)
    w2b = _block_diag2(w2).astype(BF16)
    w2k, w2v_t = w2b[0], w2b[1].T
    return pl.pallas_call(
        functools.partial(_cmp_kernel, n_cmp=n_cmp),
        grid=(batch,),
        in_specs=[pl.BlockSpec((seq_len, gw), lambda b: (b, 0)),
                  pl.BlockSpec((seq_len, gw), lambda b: (b, 1)),
                  pl.BlockSpec((2, CMP_BLOCK, gw), lambda b: (0, 0, 0)),
                  pl.BlockSpec((2, CMP_BLOCK, gw, 2 * CMP_HIDDEN), lambda b: (0, 0, 0, 0)),
                  pl.BlockSpec((2 * CMP_HIDDEN, gw), lambda b: (0, 0)),
                  pl.BlockSpec((gw, 2 * CMP_HIDDEN), lambda b: (0, 0)),
                  pl.BlockSpec((1, dh), lambda b: (0, 0))],
        out_specs=[pl.BlockSpec((None, n_cmp, gw), lambda b: (b, 0, 0)),
                   pl.BlockSpec((None, gw, n_cmp), lambda b: (b, 0, 0))],
        out_shape=[jax.ShapeDtypeStruct((batch, n_cmp, gw), BF16),
                   jax.ShapeDtypeStruct((batch, gw, n_cmp), BF16)],
        compiler_params=_params("parallel"),
        name="cmp_kv",
    )(kv, kv, pe2, w1b, w2k, w2v_t, kg)


def _softmax_pv(s, v_t):
    m = jnp.max(s, axis=0, keepdims=True)
    p = jnp.exp2(s - m)
    l = jnp.sum(p, axis=0, keepdims=True)
    return _dot(v_t, p.astype(BF16)) * (1.0 / l)


def _nsa_kernel(qt_ref, kc_ref, vct_ref, ksa_ref, vst_ref, kw_ref, vwt_ref, gate_ref, og_ref,
                o_ref, score_ref, qaug_ref, s0, s1, s_last, p0, p1, a0, a1, m_ref, l_ref, acc_ref,
                *, seq_len):
    tq, hg, dh, gw = ATT_TQ, NSA_HG, NSA_DH, NSA_KV_WIDTH
    rows = hg * tq
    n_cmp = seq_len // CMP_STRIDE
    n_blk = seq_len // SLC_BLOCK
    n_sel = min(SLC_TOPN, n_blk)
    body = SEL_BODY
    i = pl.program_id(1)
    t0 = pl.multiple_of(i * tq, tq)
    qt = qt_ref[...]

    def lane_tile(a):
        return jnp.concatenate([a] * hg, axis=1)

    def v_tiles(vt_ref, k0, tk):
        return jnp.concatenate([vt_ref[k0 // LANES + jt] for jt in range(tk // LANES)], axis=1)

    def fold8(a):
        return a.reshape(a.shape[0] // SUBLANES, SUBLANES, rows)

    span = WINDOW + tq
    w0 = pl.multiple_of(jnp.clip(t0 - WINDOW, 0, seq_len - span), tq)
    k_pos = w0 + lax.broadcasted_iota(jnp.int32, (span, tq), 0)
    q_pos = t0 + lax.broadcasted_iota(jnp.int32, (span, tq), 1)
    w_bias = jnp.where(k_pos <= q_pos, jnp.where(k_pos > q_pos - WINDOW, 0.0, NEG_INF), NEG_INF)
    s_w = _dot(kw_ref[pl.ds(w0, span), :], qt) + lane_tile(w_bias)
    o_win = _softmax_pv(s_w, v_tiles(vwt_ref, w0, span))

    c_end = lax.broadcasted_iota(jnp.int32, (n_cmp, tq), 0) * CMP_STRIDE + (CMP_BLOCK - 1)
    c_ok = c_end <= t0 + lax.broadcasted_iota(jnp.int32, (n_cmp, tq), 1)
    s_c = _dot(kc_ref[...], qt) + lane_tile(jnp.where(c_ok, 0.0, NEG_INF))
    m_c = jnp.max(s_c, axis=0, keepdims=True)
    e_c = jnp.exp2(s_c - m_c) * lane_tile(jnp.where(c_ok, 1.0, 0.0))
    l_c = jnp.sum(e_c, axis=0, keepdims=True)
    p_c = e_c * (1.0 / jnp.where(l_c > 0.0, l_c, 1.0))
    o_cmp = _dot(vct_ref[...], p_c.astype(BF16))

    p_sum = p_c[:, 0:tq]
    for h in range(1, hg):
        p_sum = p_sum + p_c[:, h * tq:(h + 1) * tq]
    blk_r = lax.broadcasted_iota(jnp.int32, (n_blk, n_cmp), 0)
    cmp_c = lax.broadcasted_iota(jnp.int32, (n_blk, n_cmp), 1)
    overlap = ((cmp_c * CMP_STRIDE < (blk_r + 1) * SLC_BLOCK)
               & (cmp_c * CMP_STRIDE + (CMP_BLOCK - 1) >= blk_r * SLC_BLOCK)
               & (cmp_c < n_cmp - 1))
    overlap = jnp.where(overlap, 1.0, 0.0).astype(BF16)
    p_hi, p_lo = _split_hi_lo(p_sum)
    imp = _dot(overlap, p_hi) + _dot(overlap, p_lo)

    blk = lax.broadcasted_iota(jnp.int32, (n_blk, tq), 0)
    t_col = t0 + lax.broadcasted_iota(jnp.int32, (n_blk, tq), 1)
    cur = t_col // SLC_BLOCK
    forced = (blk == 0) | (blk == cur) | (blk == cur - 1)
    score = jnp.where(blk * SLC_BLOCK <= t_col,
                      imp + jnp.where(forced, FORCE_BONUS, 0.0), NEG_INF)
    score_ref[...] = score

    def rank_body(jp, rank):
        row = score_ref[pl.ds(jp, 1), :]
        return rank + jnp.where(blk > jp, jnp.where(row >= score, 1.0, 0.0),
                                jnp.where(row > score, 1.0, 0.0))

    n_live = jnp.minimum((t0 + tq) // SLC_BLOCK, n_blk)
    rank = lax.fori_loop(0, n_live, rank_body, jnp.zeros((n_blk, tq), F32))
    neg = jnp.where(rank < float(n_sel), 0.0, NEG_INF)

    dead = lax.broadcasted_iota(jnp.int32, (gw - n_blk, rows), 0) == DEAD_COL - n_blk
    qaug_ref[0:gw, :] = qt
    qaug_ref[gw:gw + n_blk, :] = lane_tile(neg).astype(BF16)
    qaug_ref[gw + n_blk:2 * gw, :] = jnp.where(dead, NEG_INF, 0.0).astype(BF16)
    q_aug = qaug_ref[...]

    n_full = t0 // body
    k_last = pl.multiple_of(n_full * body, body)

    def scores(n):
        k0 = pl.multiple_of(jnp.where(n < n_full, n * body, seq_len), body)
        return _dot(ksa_ref[pl.ds(k0, body), :], q_aug)

    def pv(n, p_ref, a_ref):
        k0 = pl.multiple_of(jnp.clip(n, 0, seq_len // body - 1) * body, body)
        acc_ref[...] = a_ref[...] * acc_ref[...] + _dot(v_tiles(vst_ref, k0, body), p_ref[...])

    def softmax(s_ref, mx8, p_ref, a_ref):
        m_old = m_ref[...]
        m_new = jnp.maximum(m_old, jnp.max(mx8, axis=0, keepdims=True))
        alpha = jnp.exp2(m_old - m_new)
        p = jnp.exp2(s_ref[...] - m_new)
        l_ref[...] = alpha * l_ref[...] + jnp.sum(fold8(p), axis=0)
        m_ref[...] = m_new
        a_ref[...] = alpha
        p_ref[...] = p.astype(BF16)

    def stage(n, s_cur, mx_cur, p_prev, a_prev, p_cur, a_cur, s_next):
        pv(n - 1, p_prev, a_prev)
        softmax(s_cur, mx_cur, p_cur, a_cur)
        s = scores(n + 1)
        s_next[...] = s
        return jnp.max(fold8(s), axis=0)

    m_ref[...] = jnp.full((1, rows), NEG_INF, F32)
    l_ref[...] = jnp.zeros((SUBLANES, rows), F32)
    acc_ref[...] = jnp.zeros((dh, rows), F32)
    p1[...] = jnp.zeros((body, rows), BF16)
    a1[...] = jnp.ones((1, rows), F32)
    k_b = k_last + lax.broadcasted_iota(jnp.int32, (body, tq), 0)
    q_b = t0 + lax.broadcasted_iota(jnp.int32, (body, tq), 1)
    s = _dot(ksa_ref[pl.ds(k_last, body), :], q_aug) + lane_tile(jnp.where(k_b <= q_b, 0.0, NEG_INF))
    s_last[...] = s
    mx_last = jnp.max(fold8(s), axis=0)
    s = scores(0)
    s0[...] = s

    def sel_body(j, mx0):
        mx1 = stage(2 * j, s0, mx0, p1, a1, p0, a0, s1)
        return stage(2 * j + 1, s1, mx1, p0, a0, p1, a1, s0)

    n_trips = (n_full + 1) // 2
    lax.fori_loop(0, n_trips, sel_body, jnp.max(fold8(s), axis=0))
    pv(2 * n_trips - 1, p1, a1)
    softmax(s_last, mx_last, p0, a0)
    pv(n_full, p0, a0)
    o_sel = acc_ref[...] * (1.0 / jnp.sum(l_ref[...], axis=0, keepdims=True))

    gate_t = _sigmoid(gate_ref[...]).T
    outs = []
    for h in range(hg):
        hs = slice(h * tq, (h + 1) * tq)
        o = (gate_t[3 * h:3 * h + 1] * o_cmp[:, hs] + gate_t[3 * h + 1:3 * h + 2] * o_sel[:, hs]
             + gate_t[3 * h + 2:3 * h + 3] * o_win[:, hs])
        ms = jnp.mean(o * o, axis=0, keepdims=True)
        outs.append(o * lax.rsqrt(ms + RMS_EPS) * og_ref[h * dh:(h + 1) * dh, :])
    o_ref[...] = jnp.concatenate(outs, axis=0).T


def _nsa_attn(q_t, k_cmp, v_cmp_t, ks_aug, vs_t, kw, vw_t, small, out_g_b, batch, seq_len):
    dummy = jnp.zeros((batch, PROJ_TM, 2 * NSA_KV_WIDTH), BF16).at[:, :, NSA_KV_WIDTH + DEAD_COL].set(1.0)
    ks_aug = jnp.concatenate([ks_aug, dummy], axis=1)
    g, hg, dh, tq, gw = NSA_KV_GROUPS, NSA_HG, NSA_DH, ATT_TQ, NSA_KV_WIDTH
    nq = seq_len // tq
    n_cmp = seq_len // CMP_STRIDE
    n_blk = seq_len // SLC_BLOCK
    n_lt = seq_len // LANES
    rows = hg * tq
    assert n_blk <= DEAD_COL and tq % LANES == 0 and SEL_BODY % tq == 0 and seq_len % SEL_BODY == 0
    assert PROJ_TM >= SEL_BODY
    vt_spec = pl.BlockSpec((None, n_lt, dh, LANES), lambda b, i, j: (b, 0, j, 0))
    return pl.pallas_call(
        functools.partial(_nsa_kernel, seq_len=seq_len),
        grid=(batch, nq, g),
        in_specs=[pl.BlockSpec((None, None, None, gw, rows), lambda b, i, j: (b, i, j, 0, 0)),
                  pl.BlockSpec((None, n_cmp, gw), lambda b, i, j: (b, 0, 0)),
                  pl.BlockSpec((None, dh, n_cmp), lambda b, i, j: (b, j, 0)),
                  pl.BlockSpec((None, seq_len + PROJ_TM, 2 * gw), lambda b, i, j: (b, 0, 0)),
                  vt_spec,
                  pl.BlockSpec((None, seq_len, gw), lambda b, i, j: (b, 0, 0)),
                  vt_spec,
                  pl.BlockSpec((tq, LANES), lambda b, i, j: (b * nq + i, j)),
                  pl.BlockSpec((None, hg * dh, tq), lambda b, i, j: (j, 0, 0))],
        out_specs=pl.BlockSpec((tq, hg * dh), lambda b, i, j: (b * nq + i, j)),
        out_shape=jax.ShapeDtypeStruct((batch * seq_len, NSA_WIDTH), F32),
        scratch_shapes=[pltpu.VMEM((n_blk, tq), F32),
                        pltpu.VMEM((2 * gw, rows), BF16),
                        pltpu.VMEM((SEL_BODY, rows), F32),
                        pltpu.VMEM((SEL_BODY, rows), F32),
                        pltpu.VMEM((SEL_BODY, rows), F32),
                        pltpu.VMEM((SEL_BODY, rows), BF16),
                        pltpu.VMEM((SEL_BODY, rows), BF16),
                        pltpu.VMEM((1, rows), F32),
                        pltpu.VMEM((1, rows), F32),
                        pltpu.VMEM((1, rows), F32),
                        pltpu.VMEM((SUBLANES, rows), F32),
                        pltpu.VMEM((dh, rows), F32)],
        compiler_params=_params("parallel", "arbitrary", "arbitrary"),
        name="nsa_attn",
    )(q_t, k_cmp, v_cmp_t, ks_aug, vs_t, kw, vw_t, small, out_g_b)


def _mlstm_kernel(qk_ref, v_ref, o_ref, ift_ref, cw_ref, cb_ref, fbt_ref, ogt_ref, triu_ref,
                  y_ref, xbuf, ct_st, n_st, m_st):
    lc = M_CHUNK
    nh, dh, width = M_HEADS, M_DH, M_WIDTH
    c = pl.program_id(1)

    @pl.when(c == 0)
    def _():
        xbuf[0:SUBLANES, :] = jnp.zeros((SUBLANES, 2 * width), F32)
        ct_st[...] = jnp.zeros(ct_st.shape, F32)
        n_st[...] = jnp.zeros(n_st.shape, F32)
        m_st[...] = jnp.zeros(m_st.shape, F32)

    u = qk_ref[...]
    xbuf[SUBLANES:SUBLANES + lc, :] = u
    y = cb_ref[...] + cw_ref[0:1, :] * xbuf[pl.ds(SUBLANES - (CONV_WIDTH - 1), lc), :]
    for j in range(1, CONV_WIDTH):
        y = y + cw_ref[j:j + 1, :] * xbuf[pl.ds(SUBLANES - (CONV_WIDTH - 1) + j, lc), :]
    xbuf[0:SUBLANES, :] = u[lc - SUBLANES:lc, :]
    qk = _silu(y)

    ift = ift_ref[...]
    lf_hi, lf_lo = _split_hi_lo(_log_sigmoid(ift + fbt_ref[:, 0:1]))
    b_all = _dot(lf_hi, triu_ref[...]) + _dot(lf_lo, triu_ref[...])
    b_rows = b_all[nh:2 * nh]
    u_rows = ift[0:nh] - b_rows
    b_last = b_rows[:, lc - 1:lc]
    m_prev = m_st[0:nh, 0:1]
    g_rows = u_rows + b_last
    m_new = jnp.maximum(b_last + m_prev, jnp.max(g_rows, axis=1, keepdims=True))
    decay = jnp.exp(b_last + m_prev - m_new)
    ws_rows = jnp.exp(g_rows - m_new)

    def columns(rows):
        return jnp.concatenate([rows, jnp.zeros((LANES - nh, lc), F32)], axis=0).T

    u_cols = columns(u_rows)
    ws_cols = columns(ws_rows)
    ws_b = jnp.concatenate([ws_rows, jnp.zeros((SUBLANES - nh, lc), F32)], axis=0).astype(BF16)

    s_i = lax.broadcasted_iota(jnp.int32, (lc, lc), 0)
    t_i = lax.broadcasted_iota(jnp.int32, (lc, lc), 1)
    causal_t = s_i <= t_i
    v_all = v_ref[...]
    og_all = _sigmoid(o_ref[...])
    for h in range(nh):
        hs = slice(h * dh, (h + 1) * dh)
        q_b = qk[:, h * dh:(h + 1) * dh].astype(BF16)
        k_b = (qk[:, width + h * dh:width + (h + 1) * dh] * (dh ** -0.5)).astype(BF16)
        v_h = v_all[:, hs]
        b_row = b_rows[h:h + 1]
        ct_prev = ct_st[h]
        n_prev = n_st[h:h + 1, :]

        a = b_row + m_prev[h:h + 1]
        d_t = jnp.where(causal_t, u_cols[:, h:h + 1] + b_row, NEG_INF)
        m_t = jnp.maximum(a, jnp.max(d_t, axis=0, keepdims=True))
        w_inter = jnp.exp(a - m_t)
        w_t = jnp.exp(d_t - m_t) * _dot_nt(k_b, q_b)
        num_t = (w_inter * _dot_nt(ct_prev.astype(BF16), q_b)
                 + _dot_tn(v_h.astype(BF16), w_t.astype(BF16)))
        n_b = jnp.broadcast_to(n_prev, (SUBLANES, dh)).astype(BF16)
        den = w_inter * _dot_nt(n_b, q_b)[0:1] + jnp.sum(w_t, axis=0, keepdims=True)
        hid_t = num_t * (1.0 / jnp.maximum(jnp.abs(den), jnp.exp(-m_t)))

        dec = decay[h:h + 1]
        ct_st[h] = dec * ct_prev + _dot_tn((ws_cols[:, h:h + 1] * v_h).astype(BF16), k_b)
        n_st[h:h + 1, :] = dec * n_prev + _dot(ws_b, k_b)[h:h + 1]
        m_st[h:h + 1, :] = jnp.broadcast_to(m_new[h:h + 1], (1, LANES))

        ms = jnp.mean(hid_t * hid_t, axis=0, keepdims=True)
        y_t = hid_t * lax.rsqrt(ms + RMS_EPS) * ogt_ref[h]
        y_ref[:, hs] = og_all[:, hs] * y_t.T


def _mlstm(qk_b, v_b, o_b, if_t, conv_w, conv_b, fb_row, out_g, batch, seq_len):
    n = qk_b.shape[0]
    lc = M_CHUNK
    nc = seq_len // lc
    row = lambda b, c: (b * nc + c, 0)
    og_t = jnp.broadcast_to(out_g.reshape(M_HEADS, M_DH, 1), (M_HEADS, M_DH, lc))
    triu = (jnp.arange(lc)[:, None] <= jnp.arange(lc)[None, :]).astype(BF16)
    return pl.pallas_call(
        _mlstm_kernel,
        grid=(batch, nc),
        in_specs=[pl.BlockSpec((lc, 2 * M_WIDTH), row),
                  pl.BlockSpec((lc, M_WIDTH), row),
                  pl.BlockSpec((lc, M_WIDTH), row),
                  pl.BlockSpec((None, 2 * M_HEADS, lc), lambda b, c: (b, 0, c)),
                  pl.BlockSpec((CONV_WIDTH, 2 * M_WIDTH), lambda b, c: (0, 0)),
                  pl.BlockSpec((1, 2 * M_WIDTH), lambda b, c: (0, 0)),
                  pl.BlockSpec((2 * M_HEADS, LANES), lambda b, c: (0, 0)),
                  pl.BlockSpec((M_HEADS, M_DH, lc), lambda b, c: (0, 0, 0)),
                  pl.BlockSpec((lc, lc), lambda b, c: (0, 0))],
        out_specs=pl.BlockSpec((lc, M_WIDTH), row),
        out_shape=jax.ShapeDtypeStruct((n, M_WIDTH), F32),
        scratch_shapes=[pltpu.VMEM((lc + SUBLANES, 2 * M_WIDTH), F32),
                        pltpu.VMEM((M_HEADS, M_DH, M_DH), F32),
                        pltpu.VMEM((SUBLANES, M_DH), F32),
                        pltpu.VMEM((SUBLANES, LANES), F32)],
        compiler_params=_params("parallel", "arbitrary"),
        name="mlstm",
    )(qk_b, v_b, o_b, if_t, conv_w, conv_b, fb_row, og_t, triu)


def _out_ffn_kernel(x_ref, ya_ref, yb_ref, wo_ref, g_ref, wgu_ref, wd_ref, o_ref):
    x1 = (x_ref[...] + _dot(ya_ref[...].astype(BF16), wo_ref[0:NSA_WIDTH, :])
          + _dot(yb_ref[...].astype(BF16), wo_ref[NSA_WIDTH:, :]))
    h = _rms_rows(x1, g_ref[...]).astype(BF16)
    o_ref[...] = x1
    for c in range(FFN_HIDDEN // FFN_FC):
        cs = slice(c * FFN_FC, (c + 1) * FFN_FC)
        us = slice(FFN_HIDDEN + c * FFN_FC, FFN_HIDDEN + (c + 1) * FFN_FC)
        act = (_silu(_dot(h, wgu_ref[:, cs])) * _dot(h, wgu_ref[:, us])).astype(BF16)
        o_ref[...] += _dot(act, wd_ref[cs, :])


def _out_ffn(x2, y_a, y_b, w_out, ln_g, w_gu, w_down):
    n = x2.shape[0]
    tm = min(PROJ_TM, n)
    row = lambda i: (i, 0)
    return pl.pallas_call(
        _out_ffn_kernel,
        grid=(n // tm,),
        in_specs=[pl.BlockSpec((tm, D_MODEL), row),
                  pl.BlockSpec((tm, NSA_WIDTH), row),
                  pl.BlockSpec((tm, M_WIDTH), row),
                  _resident((D_MODEL, D_MODEL)),
                  _resident((1, D_MODEL)),
                  _resident((D_MODEL, 2 * FFN_HIDDEN)),
                  _resident((FFN_HIDDEN, D_MODEL))],
        out_specs=pl.BlockSpec((tm, D_MODEL), row),
        out_shape=jax.ShapeDtypeStruct((n, D_MODEL), F32),
        compiler_params=_params("parallel"),
        name="out_ffn",
    )(x2, y_a, y_b, w_out, ln_g[None, :], w_gu, w_down)


def _layer(x2, batch, seq_len, ln1_g, w_in, b_in, q_g, k_g, pe, w1, w2, conv_w, conv_b, fgate_b,
           nsa_out_g, m_out_g, w_out, ln2_g, w_gu, w_down):
    g, dh = NSA_KV_GROUPS, NSA_DH
    w_perm, b_perm = _permute_in_proj(w_in, b_in)
    kcv, qk_b, v_b, o_b, small, q_t, ks_aug, kw, vs_t, vw_t = _in_proj(
        x2, ln1_g, w_perm, b_perm, q_g, k_g, batch, seq_len)
    k_cmp, v_cmp_t = _compress(kcv, pe, w1, w2, k_g[0:1], batch, seq_len)
    out_g_b = jnp.broadcast_to(nsa_out_g.reshape(g, NSA_HG * dh, 1), (g, NSA_HG * dh, ATT_TQ))
    y_a = _nsa_attn(q_t, k_cmp, v_cmp_t, ks_aug, vs_t, kw, vw_t, small, out_g_b, batch, seq_len)

    if_t = small[:, LANES + I_COL:LANES + I_COL + 2 * M_HEADS].reshape(
        batch, seq_len, 2 * M_HEADS).transpose(0, 2, 1)
    fb_row = jnp.zeros((2 * M_HEADS, LANES), F32).at[M_HEADS:, :].set(
        jnp.broadcast_to(fgate_b[:, None], (M_HEADS, LANES)))
    y_b = _mlstm(qk_b, v_b, o_b, if_t, conv_w, conv_b[None, :], fb_row, m_out_g, batch, seq_len)

    return _out_ffn(x2, y_a, y_b, w_out.astype(BF16), ln2_g, w_gu.astype(BF16), w_down.astype(BF16))


def kernel(x, ln1_g, w_in, b_in, nsa_q_norm_g, nsa_k_norm_g, cmp_pe, cmp_w1, cmp_w2, m_conv_w,
           m_conv_b, m_fgate_b, nsa_out_norm_g, m_out_norm_g, w_out, ln2_g, w_gate_up, w_down):
    batch, seq_len, d = x.shape
    x2 = x.reshape(batch * seq_len, d)
    for l in range(ln1_g.shape[0]):
        x2 = _layer(x2, batch, seq_len, ln1_g[l], w_in[l], b_in[l], nsa_q_norm_g[l], nsa_k_norm_g[l],
                    cmp_pe[l], cmp_w1[l], cmp_w2[l], m_conv_w[l], m_conv_b[l], m_fgate_b[l],
                    nsa_out_norm_g[l], m_out_norm_g[l], w_out[l], ln2_g[l], w_gate_up[l], w_down[l])
    return x2.reshape(batch, seq_len, d)
```

```python
import functools

import jax
import jax.numpy as jnp
from jax import lax
from jax.experimental import pallas as pl
from jax.experimental.pallas import tpu as pltpu

F32 = jnp.float32
BF16 = jnp.bfloat16

D_MODEL = 1024
NSA_HEADS = 8
NSA_KV_GROUPS = 2
NSA_HG = NSA_HEADS // NSA_KV_GROUPS
NSA_DH = 64
NSA_WIDTH = NSA_HEADS * NSA_DH
NSA_KV_WIDTH = NSA_KV_GROUPS * NSA_DH
CMP_BLOCK = 32
CMP_STRIDE = 16
CMP_HIDDEN = 2 * NSA_DH
SLC_BLOCK = 64
SLC_TOPN = 16
WINDOW = 512
FORCE_BONUS = 1e4
M_HEADS = 4
M_WIDTH = 512
M_DH = 128
CONV_WIDTH = 4
FFN_HIDDEN = 2816
RMS_EPS = 1e-6
NEG_INF = -1e30
LOG2E = 1.4426950408889634

LANES = 128
SUBLANES = 8
VMEM_LIMIT = 56 * 1024 * 1024

PROJ_TM = 512
ATT_TQ = 256
SEL_BODY = 256
DEAD_COL = 64
M_CHUNK = 512
FFN_FC = 256

SMALL_W = 2 * LANES
GATE_COLS = 3 * NSA_HG
I_COL = GATE_COLS
F_COL = GATE_COLS + M_HEADS


def _dot(a, b):
    return jnp.dot(a, b, preferred_element_type=F32)


def _dot_nt(a, b):
    return lax.dot_general(a, b, (((1,), (1,)), ((), ())), preferred_element_type=F32)


def _dot_tn(a, b):
    return lax.dot_general(a, b, (((0,), (0,)), ((), ())), preferred_element_type=F32)


def _split_hi_lo(a):
    hi = a.astype(BF16)
    lo = (a - hi.astype(F32)).astype(BF16)
    return hi, lo


def _silu(x):
    return x * (1.0 / (1.0 + jnp.exp(-x)))


def _sigmoid(x):
    return 1.0 / (1.0 + jnp.exp(-x))


def _log_sigmoid(x):
    return jnp.minimum(x, 0.0) - jnp.log(1.0 + jnp.exp(-jnp.abs(x)))


def _rms_rows(x, g):
    ms = jnp.mean(x * x, axis=-1, keepdims=True)
    return x * lax.rsqrt(ms + RMS_EPS) * g


def _params(*sem):
    return pltpu.CompilerParams(dimension_semantics=sem, vmem_limit_bytes=VMEM_LIMIT)


def _resident(shape):
    nd = len(shape)
    return pl.BlockSpec(shape, lambda *_: (0,) * nd, pipeline_mode=pl.Buffered(1))


IN_SEGS = (("q_a", NSA_WIDTH), ("kv", 6 * NSA_KV_WIDTH), ("qk_b", 2 * M_WIDTH),
           ("v_b", M_WIDTH), ("o_b", M_WIDTH), ("small", SMALL_W))
IN_COLS = sum(w for _, w in IN_SEGS)


def _in_proj_kernel(x_ref, g_ref, w_ref, b_ref, qg_ref, kg_ref,
                    kcv_ref, qkb_ref, vb_ref, ob_ref, small_ref, qt_ref, ksa_ref, kw_ref, vst_ref, vwt_ref,
                    *, tiles_per_seq):
    x = x_ref[...]
    h = _rms_rows(x, g_ref[...]).astype(BF16)
    seg, c0 = {}, 0
    for name, width in IN_SEGS:
        seg[name] = _dot(h, w_ref[:, c0:c0 + width]) + b_ref[:, c0:c0 + width]
        c0 += width
    qkb_ref[...] = seg["qk_b"]
    vb_ref[...] = seg["v_b"]
    ob_ref[...] = seg["o_b"]
    small_ref[...] = seg["small"]
    kv = seg["kv"]
    kcv_ref[...] = kv[:, 0:2 * NSA_KV_WIDTH]
    t_base = (pl.program_id(0) % tiles_per_seq) * x.shape[0]
    _attn_prep(seg["q_a"], kv[:, 2 * NSA_KV_WIDTH:4 * NSA_KV_WIDTH], kv[:, 4 * NSA_KV_WIDTH:], t_base,
               qg_ref, kg_ref, qt_ref, ksa_ref, kw_ref, vst_ref, vwt_ref)


def _attn_prep(q, sel, win, t_base, qg_ref, kg_ref, qt_ref, ksa_ref, kw_ref, vst_ref, vwt_ref):
    tm = q.shape[0]
    dh, gw, hg = NSA_DH, NSA_KV_WIDTH, NSA_HG
    seg_r = lax.broadcasted_iota(jnp.int32, (gw, gw), 0) // dh
    seg_c = lax.broadcasted_iota(jnp.int32, (gw, gw), 1) // dh
    seg_ones = jnp.where(seg_r == seg_c, 1.0, 0.0).astype(BF16)

    def head_norm(x, g):
        hi, lo = _split_hi_lo(x * x)
        ss = _dot(hi, seg_ones) + _dot(lo, seg_ones)
        return (x * lax.rsqrt(ss * (1.0 / dh) + RMS_EPS) * g).astype(BF16)

    key = t_base + lax.broadcasted_iota(jnp.int32, (tm, gw), 0)
    col = lax.broadcasted_iota(jnp.int32, (tm, gw), 1)
    onehot = jnp.where(col == key // SLC_BLOCK, 1.0, 0.0).astype(BF16)
    ksa_ref[...] = jnp.concatenate([head_norm(sel[:, 0:gw], kg_ref[0:1, :]), onehot], axis=1)
    kw_ref[...] = head_norm(win[:, 0:gw], kg_ref[1:2, :])

    for j in range(tm // LANES):
        r = slice(j * LANES, (j + 1) * LANES)
        vst_ref[j] = sel[r, gw:].T.astype(BF16)
        vwt_ref[j] = win[r, gw:].T.astype(BF16)

    sub = ATT_TQ // LANES
    zeros = jnp.zeros((dh, hg * ATT_TQ), BF16)
    for j in range(tm // ATT_TQ):
        x_t = [q[(j * sub + u) * LANES:(j * sub + u + 1) * LANES, :].T for u in range(sub)]
        for g in range(NSA_KV_GROUPS):
            heads = []
            for h in range(hg):
                for u in range(sub):
                    seg = x_t[u][(g * hg + h) * dh:(g * hg + h + 1) * dh, :]
                    ms = jnp.mean(seg * seg, axis=0, keepdims=True)
                    heads.append(seg * lax.rsqrt(ms + RMS_EPS) * qg_ref[...] * (dh ** -0.5 * LOG2E))
            blockq = jnp.concatenate(heads, axis=1).astype(BF16)
            qt_ref[j, g] = jnp.concatenate([blockq, zeros] if g == 0 else [zeros, blockq], axis=0)


def _permute_in_proj(w_in, b_in):
    o = 0
    parts = {}
    for name, width in (("q_a", NSA_WIDTH), ("kv", 6 * NSA_KV_WIDTH), ("g_a", 3 * NSA_HEADS),
                        ("q_b", M_WIDTH), ("k_b", M_WIDTH), ("v_b", M_WIDTH),
                        ("i_b", M_HEADS), ("f_b", M_HEADS), ("o_b", M_WIDTH)):
        parts[name] = (o, o + width)
        o += width

    def perm(a):
        def sl(name):
            lo, hi = parts[name]
            return a[..., lo:hi]
        lead = a.shape[:-1]
        g_a = sl("g_a")
        small0 = jnp.concatenate([g_a[..., :GATE_COLS],
                                  jnp.zeros(lead + (LANES - GATE_COLS,), a.dtype)], axis=-1)
        small1 = jnp.concatenate([g_a[..., GATE_COLS:], sl("i_b"), sl("f_b"),
                                  jnp.zeros(lead + (LANES - GATE_COLS - 2 * M_HEADS,), a.dtype)], axis=-1)
        return jnp.concatenate([sl("q_a"), sl("kv"), sl("q_b"), sl("k_b"), sl("v_b"), sl("o_b"),
                                small0, small1], axis=-1)

    return perm(w_in).astype(BF16), perm(b_in)[None, :]


def _in_proj(x2, ln_g, w_perm, b_perm, q_g, k_g, batch, seq_len):
    n = x2.shape[0]
    tm = PROJ_TM
    nt = seq_len // tm
    gw, hg, tq = NSA_KV_WIDTH, NSA_HG, ATT_TQ
    assert seq_len % tm == 0 and tm % tq == 0
    qg_b = jnp.broadcast_to(q_g[:, None], (NSA_DH, LANES))
    kg_b = jnp.concatenate([k_g[1:3], k_g[1:3]], axis=1)
    row = lambda i: (i, 0)
    flat = lambda w: (jax.ShapeDtypeStruct((n, w), F32), pl.BlockSpec((tm, w), row))
    seq = lambda shape, blk: (jax.ShapeDtypeStruct((batch,) + shape, BF16),
                              pl.BlockSpec((None,) + blk, lambda i: (i // nt, i % nt) + (0,) * (len(blk) - 1)))
    outs = [flat(2 * gw),
            flat(2 * M_WIDTH), flat(M_WIDTH), flat(M_WIDTH), flat(SMALL_W),
            seq((seq_len // tq, NSA_KV_GROUPS, gw, hg * tq), (tm // tq, NSA_KV_GROUPS, gw, hg * tq)),
            seq((seq_len, 2 * gw), (tm, 2 * gw)),
            seq((seq_len, gw), (tm, gw)),
            seq((seq_len // LANES, gw, LANES), (tm // LANES, gw, LANES)),
            seq((seq_len // LANES, gw, LANES), (tm // LANES, gw, LANES))]
    return pl.pallas_call(
        functools.partial(_in_proj_kernel, tiles_per_seq=nt),
        grid=(n // tm,),
        in_specs=[pl.BlockSpec((tm, D_MODEL), row),
                  _resident((1, D_MODEL)),
                  _resident((D_MODEL, IN_COLS)),
                  _resident((1, IN_COLS)),
                  _resident((NSA_DH, LANES)),
                  _resident((2, gw))],
        out_specs=[o[1] for o in outs],
        out_shape=[o[0] for o in outs],
        compiler_params=_params("parallel"),
        name="in_proj",
    )(x2, ln_g[None, :], w_perm, b_perm, qg_b, kg_b)


def _cmp_kernel(k_ref, v_ref, pe_ref, w1_ref, w2k_ref, w2vt_ref, kg_ref, kc_ref, vct_ref, *, n_cmp):
    acc = [[jnp.zeros((n_cmp, 2 * CMP_HIDDEN), F32) for _ in range(2)] for _ in range(2)]
    for l in range(CMP_STRIDE):
        for w, x_ref in enumerate((k_ref, v_ref)):
            xw = x_ref[pl.ds(l, n_cmp, stride=CMP_STRIDE), :]
            for half in range(2):
                pos = half * CMP_STRIDE + l
                acc[w][half] = acc[w][half] + _dot((xw + pe_ref[w, pos:pos + 1, :]).astype(BF16),
                                                   w1_ref[w, pos])
    act = [_silu(acc[w][0] + pltpu.roll(acc[w][1], n_cmp - 1, 0)).astype(BF16) for w in range(2)]
    k_out = _dot(act[0], w2k_ref[...])
    k_out = jnp.concatenate([_rms_rows(k_out[:, g * NSA_DH:(g + 1) * NSA_DH], kg_ref[...])
                             for g in range(NSA_KV_GROUPS)], axis=1)
    kc_ref[...] = k_out.astype(BF16)
    vct_ref[...] = _dot_nt(w2vt_ref[...], act[1]).astype(BF16)


def _block_diag2(w):
    z = jnp.zeros_like(w)
    return jnp.concatenate([jnp.concatenate([w, z], axis=-1), jnp.concatenate([z, w], axis=-1)], axis=-2)


def _compress(kv, pe, w1, w2, kg, batch, seq_len):
    n_cmp = seq_len // CMP_STRIDE
    dh, gw = NSA_DH, NSA_KV_WIDTH
    pe2 = jnp.concatenate([pe, pe], axis=-1)
    w1b = _block_diag2(w1.reshape(2, CMP_BLOCK, dh, CMP_HIDDEN)).astype(BF16)
    w2b = _block_diag2(w2).astype(BF16)
    w2k, w2v_t = w2b[0], w2b[1].T
    return pl.pallas_call(
        functools.partial(_cmp_kernel, n_cmp=n_cmp),
        grid=(batch,),
        in_specs=[pl.BlockSpec((seq_len, gw), lambda b: (b, 0)),
                  pl.BlockSpec((seq_len, gw), lambda b: (b, 1)),
                  pl.BlockSpec((2, CMP_BLOCK, gw), lambda b: (0, 0, 0)),
                  pl.BlockSpec((2, CMP_BLOCK, gw, 2 * CMP_HIDDEN), lambda b: (0, 0, 0, 0)),
                  pl.BlockSpec((2 * CMP_HIDDEN, gw), lambda b: (0, 0)),
                  pl.BlockSpec((gw, 2 * CMP_HIDDEN), lambda b: (0, 0)),
                  pl.BlockSpec((1, dh), lambda b: (0, 0))],
        out_specs=[pl.BlockSpec((None, n_cmp, gw), lambda b: (b, 0, 0)),
                   pl.BlockSpec((None, gw, n_cmp), lambda b: (b, 0, 0))],
        out_shape=[jax.ShapeDtypeStruct((batch, n_cmp, gw), BF16),
                   jax.ShapeDtypeStruct((batch, gw, n_cmp), BF16)],
        compiler_params=_params("parallel"),
        name="cmp_kv",
    )(kv, kv, pe2, w1b, w2k, w2v_t, kg)


def _softmax_pv(s, v_t):
    m = jnp.max(s, axis=0, keepdims=True)
    p = jnp.exp2(s - m)
    l = jnp.sum(p, axis=0, keepdims=True)
    return _dot(v_t, p.astype(BF16)) * (1.0 / l)


def _nsa_kernel(qt_ref, kc_ref, vct_ref, ksa_ref, vst_ref, kw_ref, vwt_ref, gate_ref, og_ref,
                o_ref, score_ref, qaug_ref, s0, s1, s_last, p0, p1, a0, a1, m_ref, l_ref, acc_ref,
                *, seq_len):
    tq, hg, dh, gw = ATT_TQ, NSA_HG, NSA_DH, NSA_KV_WIDTH
    rows = hg * tq
    n_cmp = seq_len // CMP_STRIDE
    n_blk = seq_len // SLC_BLOCK
    n_sel = min(SLC_TOPN, n_blk)
    body = SEL_BODY
    i = pl.program_id(1)
    t0 = pl.multiple_of(i * tq, tq)
    qt = qt_ref[...]

    def lane_tile(a):
        return jnp.concatenate([a] * hg, axis=1)

    def v_tiles(vt_ref, k0, tk):
        return jnp.concatenate([vt_ref[k0 // LANES + jt] for jt in range(tk // LANES)], axis=1)

    def fold8(a):
        return a.reshape(a.shape[0] // SUBLANES, SUBLANES, rows)

    span = WINDOW + tq
    w0 = pl.multiple_of(jnp.clip(t0 - WINDOW, 0, seq_len - span), tq)
    k_pos = w0 + lax.broadcasted_iota(jnp.int32, (span, tq), 0)
    q_pos = t0 + lax.broadcasted_iota(jnp.int32, (span, tq), 1)
    w_bias = jnp.where(k_pos <= q_pos, jnp.where(k_pos > q_pos - WINDOW, 0.0, NEG_INF), NEG_INF)
    s_w = _dot(kw_ref[pl.ds(w0, span), :], qt) + lane_tile(w_bias)
    o_win = _softmax_pv(s_w, v_tiles(vwt_ref, w0, span))

    c_end = lax.broadcasted_iota(jnp.int32, (n_cmp, tq), 0) * CMP_STRIDE + (CMP_BLOCK - 1)
    c_ok = c_end <= t0 + lax.broadcasted_iota(jnp.int32, (n_cmp, tq), 1)
    s_c = _dot(kc_ref[...], qt) + lane_tile(jnp.where(c_ok, 0.0, NEG_INF))
    m_c = jnp.max(s_c, axis=0, keepdims=True)
    e_c = jnp.exp2(s_c - m_c)
    l_c = jnp.sum(e_c, axis=0, keepdims=True)
    tok_ok = jnp.where(t0 + lax.broadcasted_iota(jnp.int32, (1, tq), 1) >= CMP_BLOCK - 1, 1.0, 0.0)
    p_c = e_c * (lane_tile(tok_ok) * (1.0 / l_c))
    o_cmp = _dot(vct_ref[...], p_c.astype(BF16))

    p_sum = p_c[:, 0:tq]
    for h in range(1, hg):
        p_sum = p_sum + p_c[:, h * tq:(h + 1) * tq]
    blk_r = lax.broadcasted_iota(jnp.int32, (n_blk, n_cmp), 0)
    cmp_c = lax.broadcasted_iota(jnp.int32, (n_blk, n_cmp), 1)
    overlap = ((cmp_c * CMP_STRIDE < (blk_r + 1) * SLC_BLOCK)
               & (cmp_c * CMP_STRIDE + (CMP_BLOCK - 1) >= blk_r * SLC_BLOCK)
               & (cmp_c < n_cmp - 1))
    overlap = jnp.where(overlap, 1.0, 0.0).astype(BF16)
    p_hi, p_lo = _split_hi_lo(p_sum)
    imp = _dot(overlap, p_hi) + _dot(overlap, p_lo)

    blk = lax.broadcasted_iota(jnp.int32, (n_blk, tq), 0)
    t_col = t0 + lax.broadcasted_iota(jnp.int32, (n_blk, tq), 1)
    cur = t_col // SLC_BLOCK
    forced = (blk == 0) | (blk == cur) | (blk == cur - 1)
    score = jnp.where(blk * SLC_BLOCK <= t_col,
                      imp + jnp.where(forced, FORCE_BONUS, 0.0), NEG_INF)
    score_ref[...] = score

    def rank_body(jp, rank):
        row = score_ref[pl.ds(jp, 1), :]
        return rank + jnp.where(blk > jp, jnp.where(row >= score, 1.0, 0.0),
                                jnp.where(row > score, 1.0, 0.0))

    n_live = jnp.minimum((t0 + tq) // SLC_BLOCK, n_blk)
    rank = lax.fori_loop(0, n_live, rank_body, jnp.zeros((n_blk, tq), F32))
    neg = jnp.where(rank < float(n_sel), 0.0, NEG_INF)

    dead = lax.broadcasted_iota(jnp.int32, (gw - n_blk, rows), 0) == DEAD_COL - n_blk
    qaug_ref[0:gw, :] = qt
    qaug_ref[gw:gw + n_blk, :] = lane_tile(neg).astype(BF16)
    qaug_ref[gw + n_blk:2 * gw, :] = jnp.where(dead, NEG_INF, 0.0).astype(BF16)
    q_aug = qaug_ref[...]

    n_full = t0 // body
    k_last = pl.multiple_of(n_full * body, body)

    def scores(n):
        k0 = pl.multiple_of(jnp.where(n < n_full, n * body, seq_len), body)
        return _dot(ksa_ref[pl.ds(k0, body), :], q_aug)

    def pv(n, p_ref, a_ref):
        k0 = pl.multiple_of(jnp.clip(n, 0, seq_len // body - 1) * body, body)
        acc_ref[...] = a_ref[...] * acc_ref[...] + _dot(v_tiles(vst_ref, k0, body), p_ref[...])

    def softmax(s_ref, mx8, p_ref, a_ref):
        m_old = m_ref[...]
        m_new = jnp.maximum(m_old, jnp.max(mx8, axis=0, keepdims=True))
        alpha = jnp.exp2(m_old - m_new)
        p = jnp.exp2(s_ref[...] - m_new)
        l_ref[...] = alpha * l_ref[...] + jnp.sum(fold8(p), axis=0)
        m_ref[...] = m_new
        a_ref[...] = alpha
        p_ref[...] = p.astype(BF16)

    def stage(n, s_cur, mx_cur, p_prev, a_prev, p_cur, a_cur, s_next):
        pv(n - 1, p_prev, a_prev)
        softmax(s_cur, mx_cur, p_cur, a_cur)
        s = scores(n + 1)
        s_next[...] = s
        return jnp.max(fold8(s), axis=0)

    m_ref[...] = jnp.full((1, rows), NEG_INF, F32)
    l_ref[...] = jnp.zeros((SUBLANES, rows), F32)
    acc_ref[...] = jnp.zeros((dh, rows), F32)
    p1[...] = jnp.zeros((body, rows), BF16)
    a1[...] = jnp.ones((1, rows), F32)
    k_b = k_last + lax.broadcasted_iota(jnp.int32, (body, tq), 0)
    q_b = t0 + lax.broadcasted_iota(jnp.int32, (body, tq), 1)
    s = _dot(ksa_ref[pl.ds(k_last, body), :], q_aug) + lane_tile(jnp.where(k_b <= q_b, 0.0, NEG_INF))
    s_last[...] = s
    mx_last = jnp.max(fold8(s), axis=0)
    s = scores(0)
    s0[...] = s

    def sel_body(j, mx0):
        mx1 = stage(2 * j, s0, mx0, p1, a1, p0, a0, s1)
        return stage(2 * j + 1, s1, mx1, p0, a0, p1, a1, s0)

    n_trips = (n_full + 1) // 2
    lax.fori_loop(0, n_trips, sel_body, jnp.max(fold8(s), axis=0))
    pv(2 * n_trips - 1, p1, a1)
    softmax(s_last, mx_last, p0, a0)
    pv(n_full, p0, a0)
    o_sel = acc_ref[...] * (1.0 / jnp.sum(l_ref[...], axis=0, keepdims=True))

    gate_t = _sigmoid(gate_ref[...]).T
    outs = []
    for h in range(hg):
        hs = slice(h * tq, (h + 1) * tq)
        o = (gate_t[3 * h:3 * h + 1] * o_cmp[:, hs] + gate_t[3 * h + 1:3 * h + 2] * o_sel[:, hs]
             + gate_t[3 * h + 2:3 * h + 3] * o_win[:, hs])
        ms = jnp.mean(o * o, axis=0, keepdims=True)
        outs.append(o * lax.rsqrt(ms + RMS_EPS) * og_ref[h * dh:(h + 1) * dh, :])
    o_ref[...] = jnp.concatenate(outs, axis=0).T


def _nsa_attn(q_t, k_cmp, v_cmp_t, ks_aug, vs_t, kw, vw_t, small, out_g_b, batch, seq_len):
    dummy = jnp.zeros((batch, PROJ_TM, 2 * NSA_KV_WIDTH), BF16).at[:, :, NSA_KV_WIDTH + DEAD_COL].set(1.0)
    ks_aug = jnp.concatenate([ks_aug, dummy], axis=1)
    g, hg, dh, tq, gw = NSA_KV_GROUPS, NSA_HG, NSA_DH, ATT_TQ, NSA_KV_WIDTH
    nq = seq_len // tq
    n_cmp = seq_len // CMP_STRIDE
    n_blk = seq_len // SLC_BLOCK
    n_lt = seq_len // LANES
    rows = hg * tq
    assert n_blk <= DEAD_COL and tq % LANES == 0 and SEL_BODY % tq == 0 and seq_len % SEL_BODY == 0
    assert PROJ_TM >= SEL_BODY
    vt_spec = pl.BlockSpec((None, n_lt, dh, LANES), lambda b, i, j: (b, 0, j, 0))
    return pl.pallas_call(
        functools.partial(_nsa_kernel, seq_len=seq_len),
        grid=(batch, nq, g),
        in_specs=[pl.BlockSpec((None, None, None, gw, rows), lambda b, i, j: (b, i, j, 0, 0)),
                  pl.BlockSpec((None, n_cmp, gw), lambda b, i, j: (b, 0, 0)),
                  pl.BlockSpec((None, dh, n_cmp), lambda b, i, j: (b, j, 0)),
                  pl.BlockSpec((None, seq_len + PROJ_TM, 2 * gw), lambda b, i, j: (b, 0, 0)),
                  vt_spec,
                  pl.BlockSpec((None, seq_len, gw), lambda b, i, j: (b, 0, 0)),
                  vt_spec,
                  pl.BlockSpec((tq, LANES), lambda b, i, j: (b * nq + i, j)),
                  pl.BlockSpec((None, hg * dh, tq), lambda b, i, j: (j, 0, 0))],
        out_specs=pl.BlockSpec((tq, hg * dh), lambda b, i, j: (b * nq + i, j)),
        out_shape=jax.ShapeDtypeStruct((batch * seq_len, NSA_WIDTH), F32),
        scratch_shapes=[pltpu.VMEM((n_blk, tq), F32),
                        pltpu.VMEM((2 * gw, rows), BF16),
                        pltpu.VMEM((SEL_BODY, rows), F32),
                        pltpu.VMEM((SEL_BODY, rows), F32),
                        pltpu.VMEM((SEL_BODY, rows), F32),
                        pltpu.VMEM((SEL_BODY, rows), BF16),
                        pltpu.VMEM((SEL_BODY, rows), BF16),
                        pltpu.VMEM((1, rows), F32),
                        pltpu.VMEM((1, rows), F32),
                        pltpu.VMEM((1, rows), F32),
                        pltpu.VMEM((SUBLANES, rows), F32),
                        pltpu.VMEM((dh, rows), F32)],
        compiler_params=_params("parallel", "arbitrary", "arbitrary"),
        name="nsa_attn",
    )(q_t, k_cmp, v_cmp_t, ks_aug, vs_t, kw, vw_t, small, out_g_b)


def _mlstm_kernel(qk_ref, v_ref, o_ref, ift_ref, cw_ref, cb_ref, fbt_ref, ogt_ref, triu_ref,
                  y_ref, xbuf, ct_st, n_st, m_st):
    lc = M_CHUNK
    nh, dh, width = M_HEADS, M_DH, M_WIDTH
    c = pl.program_id(1)

    @pl.when(c == 0)
    def _():
        xbuf[0:SUBLANES, :] = jnp.zeros((SUBLANES, 2 * width), F32)
        ct_st[...] = jnp.zeros(ct_st.shape, F32)
        n_st[...] = jnp.zeros(n_st.shape, F32)
        m_st[...] = jnp.zeros(m_st.shape, F32)

    u = qk_ref[...]
    xbuf[SUBLANES:SUBLANES + lc, :] = u
    y = cb_ref[...] + cw_ref[0:1, :] * xbuf[pl.ds(SUBLANES - (CONV_WIDTH - 1), lc), :]
    for j in range(1, CONV_WIDTH):
        y = y + cw_ref[j:j + 1, :] * xbuf[pl.ds(SUBLANES - (CONV_WIDTH - 1) + j, lc), :]
    xbuf[0:SUBLANES, :] = u[lc - SUBLANES:lc, :]
    qk = _silu(y)

    ift = ift_ref[...]
    lf_hi, lf_lo = _split_hi_lo(_log_sigmoid(ift + fbt_ref[:, 0:1]))
    b_all = _dot(lf_hi, triu_ref[...]) + _dot(lf_lo, triu_ref[...])
    b_rows = b_all[nh:2 * nh]
    u_rows = ift[0:nh] - b_rows
    b_last = b_rows[:, lc - 1:lc]
    m_prev = m_st[0:nh, 0:1]
    g_rows = u_rows + b_last
    m_new = jnp.maximum(b_last + m_prev, jnp.max(g_rows, axis=1, keepdims=True))
    decay = jnp.exp(b_last + m_prev - m_new)
    ws_rows = jnp.exp(g_rows - m_new)

    def columns(rows):
        return jnp.concatenate([rows, jnp.zeros((LANES - nh, lc), F32)], axis=0).T

    u_cols = columns(u_rows)
    ws_cols = columns(ws_rows)
    ws_b = jnp.concatenate([ws_rows, jnp.zeros((SUBLANES - nh, lc), F32)], axis=0).astype(BF16)

    s_i = lax.broadcasted_iota(jnp.int32, (lc, lc), 0)
    t_i = lax.broadcasted_iota(jnp.int32, (lc, lc), 1)
    causal_t = s_i <= t_i
    v_all = v_ref[...]
    og_all = _sigmoid(o_ref[...])
    for h in range(nh):
        hs = slice(h * dh, (h + 1) * dh)
        q_b = qk[:, h * dh:(h + 1) * dh].astype(BF16)
        k_b = (qk[:, width + h * dh:width + (h + 1) * dh] * (dh ** -0.5)).astype(BF16)
        v_h = v_all[:, hs]
        b_row = b_rows[h:h + 1]
        ct_prev = ct_st[h]
        n_prev = n_st[h:h + 1, :]

        a = b_row + m_prev[h:h + 1]
        d_t = jnp.where(causal_t, u_cols[:, h:h + 1] + b_row, NEG_INF)
        m_t = jnp.maximum(a, jnp.max(d_t, axis=0, keepdims=True))
        w_inter = jnp.exp(a - m_t)
        w_t = jnp.exp(d_t - m_t) * _dot_nt(k_b, q_b)
        num_t = (w_inter * _dot_nt(ct_prev.astype(BF16), q_b)
                 + _dot_tn(v_h.astype(BF16), w_t.astype(BF16)))
        n_b = jnp.broadcast_to(n_prev, (SUBLANES, dh)).astype(BF16)
        den = w_inter * _dot_nt(n_b, q_b)[0:1] + jnp.sum(w_t, axis=0, keepdims=True)
        hid_t = num_t * (1.0 / jnp.maximum(jnp.abs(den), jnp.exp(-m_t)))

        dec = decay[h:h + 1]
        ct_st[h] = dec * ct_prev + _dot_tn((ws_cols[:, h:h + 1] * v_h).astype(BF16), k_b)
        n_st[h:h + 1, :] = dec * n_prev + _dot(ws_b, k_b)[h:h + 1]
        m_st[h:h + 1, :] = jnp.broadcast_to(m_new[h:h + 1], (1, LANES))

        ms = jnp.mean(hid_t * hid_t, axis=0, keepdims=True)
        y_t = hid_t * lax.rsqrt(ms + RMS_EPS) * ogt_ref[h]
        y_ref[:, hs] = og_all[:, hs] * y_t.T


def _mlstm(qk_b, v_b, o_b, if_t, conv_w, conv_b, fb_row, out_g, batch, seq_len):
    n = qk_b.shape[0]
    lc = M_CHUNK
    nc = seq_len // lc
    row = lambda b, c: (b * nc + c, 0)
    og_t = jnp.broadcast_to(out_g.reshape(M_HEADS, M_DH, 1), (M_HEADS, M_DH, lc))
    triu = (jnp.arange(lc)[:, None] <= jnp.arange(lc)[None, :]).astype(BF16)
    return pl.pallas_call(
        _mlstm_kernel,
        grid=(batch, nc),
        in_specs=[pl.BlockSpec((lc, 2 * M_WIDTH), row),
                  pl.BlockSpec((lc, M_WIDTH), row),
                  pl.BlockSpec((lc, M_WIDTH), row),
                  pl.BlockSpec((None, 2 * M_HEADS, lc), lambda b, c: (b, 0, c)),
                  pl.BlockSpec((CONV_WIDTH, 2 * M_WIDTH), lambda b, c: (0, 0)),
                  pl.BlockSpec((1, 2 * M_WIDTH), lambda b, c: (0, 0)),
                  pl.BlockSpec((2 * M_HEADS, LANES), lambda b, c: (0, 0)),
                  pl.BlockSpec((M_HEADS, M_DH, lc), lambda b, c: (0, 0, 0)),
                  pl.BlockSpec((lc, lc), lambda b, c: (0, 0))],
        out_specs=pl.BlockSpec((lc, M_WIDTH), row),
        out_shape=jax.ShapeDtypeStruct((n, M_WIDTH), F32),
        scratch_shapes=[pltpu.VMEM((lc + SUBLANES, 2 * M_WIDTH), F32),
                        pltpu.VMEM((M_HEADS, M_DH, M_DH), F32),
                        pltpu.VMEM((SUBLANES, M_DH), F32),
                        pltpu.VMEM((SUBLANES, LANES), F32)],
        compiler_params=_params("parallel", "arbitrary"),
        name="mlstm",
    )(qk_b, v_b, o_b, if_t, conv_w, conv_b, fb_row, og_t, triu)


def _out_ffn_kernel(x_ref, ya_ref, yb_ref, wo_ref, g_ref, wgu_ref, wd_ref, o_ref):
    x1 = (x_ref[...] + _dot(ya_ref[...].astype(BF16), wo_ref[0:NSA_WIDTH, :])
          + _dot(yb_ref[...].astype(BF16), wo_ref[NSA_WIDTH:, :]))
    h = _rms_rows(x1, g_ref[...]).astype(BF16)
    o_ref[...] = x1
    for c in range(FFN_HIDDEN // FFN_FC):
        cs = slice(c * FFN_FC, (c + 1) * FFN_FC)
        us = slice(FFN_HIDDEN + c * FFN_FC, FFN_HIDDEN + (c + 1) * FFN_FC)
        act = (_silu(_dot(h, wgu_ref[:, cs])) * _dot(h, wgu_ref[:, us])).astype(BF16)
        o_ref[...] += _dot(act, wd_ref[cs, :])


def _out_ffn(x2, y_a, y_b, w_out, ln_g, w_gu, w_down):
    n = x2.shape[0]
    tm = min(PROJ_TM, n)
    row = lambda i: (i, 0)
    return pl.pallas_call(
        _out_ffn_kernel,
        grid=(n // tm,),
        in_specs=[pl.BlockSpec((tm, D_MODEL), row),
                  pl.BlockSpec((tm, NSA_WIDTH), row),
                  pl.BlockSpec((tm, M_WIDTH), row),
                  _resident((D_MODEL, D_MODEL)),
                  _resident((1, D_MODEL)),
                  _resident((D_MODEL, 2 * FFN_HIDDEN)),
                  _resident((FFN_HIDDEN, D_MODEL))],
        out_specs=pl.BlockSpec((tm, D_MODEL), row),
        out_shape=jax.ShapeDtypeStruct((n, D_MODEL), F32),
        compiler_params=_params("parallel"),
        name="out_ffn",
    )(x2, y_a, y_b, w_out, ln_g[None, :], w_gu, w_down)


def _layer(x2, batch, seq_len, ln1_g, w_in, b_in, q_g, k_g, pe, w1, w2, conv_w, conv_b, fgate_b,
           nsa_out_g, m_out_g, w_out, ln2_g, w_gu, w_down):
    g, dh = NSA_KV_GROUPS, NSA_DH
    w_perm, b_perm = _permute_in_proj(w_in, b_in)
    kcv, qk_b, v_b, o_b, small, q_t, ks_aug, kw, vs_t, vw_t = _in_proj(
        x2, ln1_g, w_perm, b_perm, q_g, k_g, batch, seq_len)
    k_cmp, v_cmp_t = _compress(kcv, pe, w1, w2, k_g[0:1], batch, seq_len)
    out_g_b = jnp.broadcast_to(nsa_out_g.reshape(g, NSA_HG * dh, 1), (g, NSA_HG * dh, ATT_TQ))
    y_a = _nsa_attn(q_t, k_cmp, v_cmp_t, ks_aug, vs_t, kw, vw_t, small, out_g_b, batch, seq_len)

    if_t = small[:, LANES + I_COL:LANES + I_COL + 2 * M_HEADS].reshape(
        batch, seq_len, 2 * M_HEADS).transpose(0, 2, 1)
    fb_row = jnp.zeros((2 * M_HEADS, LANES), F32).at[M_HEADS:, :].set(
        jnp.broadcast_to(fgate_b[:, None], (M_HEADS, LANES)))
    y_b = _mlstm(qk_b, v_b, o_b, if_t, conv_w, conv_b[None, :], fb_row, m_out_g, batch, seq_len)

    return _out_ffn(x2, y_a, y_b, w_out.astype(BF16), ln2_g, w_gu.astype(BF16), w_down.astype(BF16))


def kernel(x, ln1_g, w_in, b_in, nsa_q_norm_g, nsa_k_norm_g, cmp_pe, cmp_w1, cmp_w2, m_conv_w,
           m_conv_b, m_fgate_b, nsa_out_norm_g, m_out_norm_g, w_out, ln2_g, w_gate_up, w_down):
    batch, seq_len, d = x.shape
    x2 = x.reshape(batch * seq_len, d)
    for l in range(ln1_g.shape[0]):
        x2 = _layer(x2, batch, seq_len, ln1_g[l], w_in[l], b_in[l], nsa_q_norm_g[l], nsa_k_norm_g[l],
                    cmp_pe[l], cmp_w1[l], cmp_w2[l], m_conv_w[l], m_conv_b[l], m_fgate_b[l],
                    nsa_out_norm_g[l], m_out_norm_g[l], w_out[l], ln2_g[l], w_gate_up[l], w_down[l])
    return x2.reshape(batch, seq_len, d)
```
